```python
import math
import jax, jax.numpy as jnp
from jax import lax
import numpy as np

D_MODEL = 1024
BATCH = 8
SEQ = 2048
DEPTH = 1

N_ATTN_HEADS = 8
ATTN_HEAD_DIM = 64
ATTN_V_DIM = 2 * ATTN_HEAD_DIM
QK_WIDTH = N_ATTN_HEADS * 2 * ATTN_HEAD_DIM
ATTN_WIDTH = N_ATTN_HEADS * ATTN_V_DIM
Q_BLOCK = 128
REC_WIDTH = 1024
REC_BLOCKS = 16
REC_BLOCK_DIM = REC_WIDTH // REC_BLOCKS
CONV_WIDTH = 4
RG_LRU_C = 8.0
N_BRANCHES = 2
D_FF = -(-8 * D_MODEL // (3 * 256)) * 256
LN_EPS = 1e-5
ALPHA = (2.0 * DEPTH) ** 0.25
BETA = (8.0 * DEPTH) ** -0.25
Q_OFF = 0
K_OFF = Q_OFF + QK_WIDTH
V_OFF = K_OFF + QK_WIDTH
XR_OFF = V_OFF + ATTN_WIDTH
YR_OFF = XR_OFF + REC_WIDTH
G_OFF = YR_OFF + REC_WIDTH
IN_WIDTH = G_OFF + N_BRANCHES * D_MODEL

kernel_name = "hybrid_diffattn_rglru_deepnorm"


def layer_norm(x, g, b):
    xf = x.astype(jnp.float32)
    mu = jnp.mean(xf, axis=-1, keepdims=True)
    xc = xf - mu
    var = jnp.mean(xc * xc, axis=-1, keepdims=True)
    y = xc * lax.rsqrt(var + LN_EPS) * g.astype(jnp.float32) + b.astype(jnp.float32)
    return y.astype(x.dtype)


def diff_attention(q, k, v, lam):
    B, S = q.shape[0], q.shape[1]
    nblk = S // Q_BLOCK
    scale = ATTN_HEAD_DIM ** -0.5
    slopes = 2.0 ** (-8.0 * jnp.arange(1, N_ATTN_HEADS + 1, dtype=jnp.float32) / N_ATTN_HEADS)
    kpos = jnp.arange(S)
    qb = q.reshape(B, nblk, Q_BLOCK, N_ATTN_HEADS, 2, ATTN_HEAD_DIM).transpose(1, 0, 2, 3, 4, 5)

    def one_block(args):
        blk, q_blk = args
        qpos = blk * Q_BLOCK + jnp.arange(Q_BLOCK)
        dist = (qpos[:, None] - kpos[None, :]).astype(jnp.float32)
        bias = jnp.where(dist[None] >= 0, -slopes[:, None, None] * dist[None], -jnp.inf)
        s = jnp.einsum('bqhcd,bkhcd->bhcqk', q_blk, k,
                       preferred_element_type=jnp.float32) * scale + bias[None, :, None]
        p = jax.nn.softmax(s, axis=-1)
        a = p[:, :, 0] - lam * p[:, :, 1]
        return jnp.einsum('bhqk,bkhe->bqhe', a.astype(v.dtype), v,
                          preferred_element_type=jnp.float32)

    o = lax.map(one_block, (jnp.arange(nblk), qb))
    return o.transpose(1, 0, 2, 3, 4).reshape(B, S, N_ATTN_HEADS, ATTN_V_DIM)


def causal_depthwise_conv(x, w, b):
    y = lax.conv_general_dilated(x, w[:, None, :], window_strides=(1,),
                                 padding=[(CONV_WIDTH - 1, 0)],
                                 dimension_numbers=('NWC', 'WIO', 'NWC'),
                                 feature_group_count=x.shape[-1])
    return y + b


def rg_lru(xr, w_a, b_a, w_i, b_i, lru_lambda):
    B, S, _ = xr.shape
    xb = xr.reshape(B, S, REC_BLOCKS, REC_BLOCK_DIM)
    r = jax.nn.sigmoid((jnp.einsum('bsni,nij->bsnj', xb, w_a).reshape(B, S, REC_WIDTH) + b_a).astype(jnp.float32))
    i = jax.nn.sigmoid((jnp.einsum('bsni,nij->bsnj', xb, w_i).reshape(B, S, REC_WIDTH) + b_i).astype(jnp.float32))
    log_a = -RG_LRU_C * r * jax.nn.softplus(-lru_lambda.astype(jnp.float32))
    a = jnp.exp(log_a)
    u = jnp.sqrt(-jnp.expm1(2.0 * log_a)) * (i * xr.astype(jnp.float32))

    def combine(left, right):
        a_l, h_l = left
        a_r, h_r = right
        return a_l * a_r, a_r * h_l + h_r

    _, h = lax.associative_scan(combine, (a, u), axis=1)
    return h


def hybrid_mixer(x, w_in, b_gate, lambda_q1, lambda_k1, lambda_q2, lambda_k2, subln_g,
                 lam_init, conv_w, conv_b, w_a, b_a, w_i, b_i, lru_lambda,
                 w_br_attn, w_br_rec, w_out):
    B, S, _ = x.shape
    z = x @ w_in
    q, k, v, xr, yr, gl = jnp.split(z, [K_OFF, V_OFF, XR_OFF, YR_OFF, G_OFF], axis=-1)
    q = q.reshape(B, S, N_ATTN_HEADS, 2, ATTN_HEAD_DIM)
    k = k.reshape(B, S, N_ATTN_HEADS, 2, ATTN_HEAD_DIM)
    v = v.reshape(B, S, N_ATTN_HEADS, ATTN_V_DIM)
    lam = (jnp.exp(jnp.sum(lambda_q1.astype(jnp.float32) * lambda_k1.astype(jnp.float32)))
           - jnp.exp(jnp.sum(lambda_q2.astype(jnp.float32) * lambda_k2.astype(jnp.float32)))
           + lam_init)
    o = diff_attention(q, k, v, lam)
    o = o * lax.rsqrt(jnp.mean(o * o, axis=-1, keepdims=True) + LN_EPS)
    o = o * subln_g.astype(jnp.float32) * (1.0 - lam_init)
    attn_out = o.reshape(B, S, ATTN_WIDTH).astype(x.dtype)
    xc = causal_depthwise_conv(xr, conv_w, conv_b)
    h = rg_lru(xc, w_a, b_a, w_i, b_i, lru_lambda)
    rec_out = (h * jax.nn.gelu(yr.astype(jnp.float32))).astype(x.dtype)
    g = jax.nn.sigmoid((gl + b_gate).astype(jnp.float32)).reshape(B, S, N_BRANCHES, D_MODEL).astype(x.dtype)
    merged = g[:, :, 0] * (attn_out @ w_br_attn) + g[:, :, 1] * (rec_out @ w_br_rec)
    return merged @ w_out


def swiglu(x, w_gate, w_up, w_down):
    return (jax.nn.silu(x @ w_gate) * (x @ w_up)) @ w_down


def setup_inputs(seed: int = 0) -> dict:
    key = jax.random.key(seed)
    ks = jax.random.split(key, 26)
    f32 = jnp.float32
    nrm = lambda k, shape, s: jax.random.normal(k, shape, f32) * s
    col_scale = jnp.ones((IN_WIDTH,), f32).at[V_OFF:V_OFF + ATTN_WIDTH].set(BETA)
    u = jax.random.uniform(ks[14], (DEPTH, REC_WIDTH), f32, minval=0.9, maxval=0.999)
    a0 = u ** (1.0 / RG_LRU_C)
    lru_lambda = jnp.log(a0) - jnp.log1p(-a0)
    return {
        "x": nrm(ks[0], (BATCH, SEQ, D_MODEL), 1.0),
        "w_in": nrm(ks[1], (DEPTH, D_MODEL, IN_WIDTH), D_MODEL ** -0.5) * col_scale,
        "b_gate": nrm(ks[2], (DEPTH, N_BRANCHES * D_MODEL), 0.01),
        "lambda_q1": nrm(ks[3], (DEPTH, ATTN_HEAD_DIM), 0.1),
        "lambda_k1": nrm(ks[4], (DEPTH, ATTN_HEAD_DIM), 0.1),
        "lambda_q2": nrm(ks[5], (DEPTH, ATTN_HEAD_DIM), 0.1),
        "lambda_k2": nrm(ks[6], (DEPTH, ATTN_HEAD_DIM), 0.1),
        "subln_g": 1.0 + nrm(ks[7], (DEPTH, ATTN_V_DIM), 0.02),
        "conv_w": nrm(ks[8], (DEPTH, CONV_WIDTH, REC_WIDTH), CONV_WIDTH ** -0.5),
        "conv_b": nrm(ks[9], (DEPTH, REC_WIDTH), 0.01),
        "w_a": nrm(ks[10], (DEPTH, REC_BLOCKS, REC_BLOCK_DIM, REC_BLOCK_DIM), REC_BLOCK_DIM ** -0.5),
        "b_a": nrm(ks[11], (DEPTH, REC_WIDTH), 0.01),
        "w_i": nrm(ks[12], (DEPTH, REC_BLOCKS, REC_BLOCK_DIM, REC_BLOCK_DIM), REC_BLOCK_DIM ** -0.5),
        "b_i": nrm(ks[13], (DEPTH, REC_WIDTH), 0.01),
        "lru_lambda": lru_lambda,
        "w_br_attn": nrm(ks[15], (DEPTH, ATTN_WIDTH, D_MODEL), ATTN_WIDTH ** -0.5 * BETA),
        "w_br_rec": nrm(ks[16], (DEPTH, REC_WIDTH, D_MODEL), REC_WIDTH ** -0.5 * BETA),
        "w_out": nrm(ks[17], (DEPTH, D_MODEL, D_MODEL), D_MODEL ** -0.5 * BETA),
        "ln1_g": 1.0 + nrm(ks[18], (DEPTH, D_MODEL), 0.02),
        "ln1_b": nrm(ks[19], (DEPTH, D_MODEL), 0.01),
        "w_gate": nrm(ks[20], (DEPTH, D_MODEL, D_FF), D_MODEL ** -0.5 * BETA),
        "w_up": nrm(ks[21], (DEPTH, D_MODEL, D_FF), D_MODEL ** -0.5 * BETA),
        "w_down": nrm(ks[22], (DEPTH, D_FF, D_MODEL), D_FF ** -0.5 * BETA),
        "ln2_g": 1.0 + nrm(ks[23], (DEPTH, D_MODEL), 0.02),
        "ln2_b": nrm(ks[24], (DEPTH, D_MODEL), 0.01),
    }


def reference(x, w_in, b_gate, lambda_q1, lambda_k1, lambda_q2, lambda_k2, subln_g,
              conv_w, conv_b, w_a, b_a, w_i, b_i, lru_lambda, w_br_attn, w_br_rec, w_out,
              ln1_g, ln1_b, w_gate, w_up, w_down, ln2_g, ln2_b):
    for l in range(DEPTH):
        lam_init = 0.8 - 0.6 * math.exp(-0.3 * l)
        mix = hybrid_mixer(x, w_in[l], b_gate[l], lambda_q1[l], lambda_k1[l], lambda_q2[l],
                           lambda_k2[l], subln_g[l], lam_init, conv_w[l], conv_b[l],
                           w_a[l], b_a[l], w_i[l], b_i[l], lru_lambda[l],
                           w_br_attn[l], w_br_rec[l], w_out[l])
        x = layer_norm(ALPHA * x + mix, ln1_g[l], ln1_b[l])
        x = layer_norm(ALPHA * x + swiglu(x, w_gate[l], w_up[l], w_down[l]), ln2_g[l], ln2_b[l])
    return x
```

```python
import functools
import math

import jax
import jax.numpy as jnp
from jax import lax
from jax.experimental import pallas as pl
from jax.experimental.pallas import tpu as pltpu

D_MODEL = 1024
N_HEADS = 8
HEAD_DIM = 64
V_DIM = 2 * HEAD_DIM
QK_WIDTH = N_HEADS * 2 * HEAD_DIM
ATTN_WIDTH = N_HEADS * V_DIM
REC_WIDTH = 1024
REC_BLOCKS = 16
REC_BLOCK_DIM = REC_WIDTH // REC_BLOCKS
CONV_WIDTH = 4
RG_LRU_C = 8.0
D_FF = 2816
LN_EPS = 1e-5
DEPTH = 1
ALPHA = (2.0 * DEPTH) ** 0.25
Q_OFF = 0
K_OFF = Q_OFF + QK_WIDTH
V_OFF = K_OFF + QK_WIDTH
XR_OFF = V_OFF + ATTN_WIDTH
YR_OFF = XR_OFF + REC_WIDTH
G_OFF = YR_OFF + REC_WIDTH
IN_WIDTH = G_OFF + 2 * D_MODEL

V7X_LANES = 128
V7X_SUBLANES = 8
V7X_MXU_DIM = 256
V7X_VMEM_BYTES = 64 * 1024 * 1024

GATE_TILE = V7X_MXU_DIM
GATE_TILES = REC_WIDTH // GATE_TILE
BLOCKS_PER_TILE = GATE_TILE // REC_BLOCK_DIM

SEQ_TILE = 256
Q_TILE = 256
KV_TILE = 256
ROW_TILE = 256
FF_CHUNK = 1408

BF16 = jnp.bfloat16
F32 = jnp.float32


def _dot(a, b):
    return jnp.dot(a, b, preferred_element_type=F32)


def _resident(shape):
    nd = len(shape)
    return pl.BlockSpec(shape, lambda *_: (0,) * nd, pipeline_mode=pl.Buffered(1))


def _in_proj_recurrent_kernel(x_ref, w_ref, bgate_ref, convw_ref, convb_ref, wa_ref, ba_ref,
                              wi_ref, bi_ref, lam_ref,
                              q_ref, k_ref, v_ref, rec_ref, g_ref,
                              xr_carry, h_carry):
    ts = x_ref.shape[0]

    @pl.when(pl.program_id(1) == 0)
    def _():
        xr_carry[...] = jnp.zeros_like(xr_carry)
        h_carry[...] = jnp.zeros_like(h_carry)

    xb = x_ref[...].astype(BF16)

    q_ref[...] = (_dot(xb, w_ref[:, Q_OFF:K_OFF]) * (HEAD_DIM ** -0.5)).astype(BF16)
    k_ref[...] = _dot(xb, w_ref[:, K_OFF:V_OFF]).astype(BF16)
    v_ref[...] = _dot(xb, w_ref[:, V_OFF:XR_OFF]).astype(BF16)

    xr = _dot(xb, w_ref[:, XR_OFF:YR_OFF])
    prev = xr_carry[...]
    xr_carry[...] = xr[ts - V7X_SUBLANES:, :]
    row8 = lax.broadcasted_iota(jnp.int32, (V7X_SUBLANES, REC_WIDTH), 0)
    xc = xr * convw_ref[CONV_WIDTH - 1:CONV_WIDTH, :] + convb_ref[...]
    for shift in range(1, CONV_WIDTH):
        rolled = pltpu.roll(xr, shift, 0)
        head = jnp.where(row8 < shift, pltpu.roll(prev, shift, 0), rolled[:V7X_SUBLANES, :])
        shifted = jnp.concatenate([head, rolled[V7X_SUBLANES:, :]], axis=0)
        xc = xc + shifted * convw_ref[CONV_WIDTH - 1 - shift:CONV_WIDTH - shift, :]

    xcb = xc.astype(BF16)
    ga, gi = [], []
    for t in range(GATE_TILES):
        cols = slice(t * GATE_TILE, (t + 1) * GATE_TILE)
        ga.append(_dot(xcb[:, cols], wa_ref[t]))
        gi.append(_dot(xcb[:, cols], wi_ref[t]))
    r = jax.nn.sigmoid(jnp.concatenate(ga, axis=1) + ba_ref[...])
    gate_i = jax.nn.sigmoid(jnp.concatenate(gi, axis=1) + bi_ref[...])

    neg_lam = -lam_ref[...]
    softplus = jnp.maximum(neg_lam, 0.0) + jnp.log1p(jnp.exp(-jnp.abs(neg_lam)))
    log_a = (-RG_LRU_C) * r * softplus
    a = jnp.exp(log_a)
    u = jnp.sqrt(-jnp.tanh(log_a) * (a * a + 1.0)) * (gate_i * xc)

    h_prev = h_carry[...]
    hs = []
    for grp in range(ts // V7X_SUBLANES):
        rows = slice(grp * V7X_SUBLANES, (grp + 1) * V7X_SUBLANES)
        ag, hg = a[rows, :], u[rows, :]
        for d in (1, 2, 4):
            keep = row8 >= d
            a_s = jnp.where(keep, pltpu.roll(ag, d, 0), 1.0)
            h_s = jnp.where(keep, pltpu.roll(hg, d, 0), 0.0)
            hg = hg + ag * h_s
            ag = ag * a_s
        hg = hg + ag * h_prev
        h_prev = hg[V7X_SUBLANES - 1:, :]
        hs.append(hg)
    h_carry[...] = h_prev
    h = jnp.concatenate(hs, axis=0)

    yr = _dot(xb, w_ref[:, YR_OFF:G_OFF])
    rec_ref[...] = (h * jax.nn.gelu(yr)).astype(BF16)

    gl = _dot(xb, w_ref[:, G_OFF:IN_WIDTH])
    g_ref[...] = jax.nn.sigmoid(gl + bgate_ref[...])


def _in_proj_recurrent(x, w_in, b_gate, conv_w, conv_b, wa, b_a, wi, b_i, lru_lambda):
    B, S, D = x.shape
    ts = SEQ_TILE
    row = lambda width: pl.BlockSpec((None, ts, width), lambda b, i: (b, i, 0))
    out_shape = (
        jax.ShapeDtypeStruct((B, S, QK_WIDTH), BF16),
        jax.ShapeDtypeStruct((B, S, QK_WIDTH), BF16),
        jax.ShapeDtypeStruct((B, S, ATTN_WIDTH), BF16),
        jax.ShapeDtypeStruct((B, S, REC_WIDTH), BF16),
        jax.ShapeDtypeStruct((B, S, 2 * D_MODEL), F32),
    )
    return pl.pallas_call(
        _in_proj_recurrent_kernel,
        grid=(B, S // ts),
        in_specs=[
            row(D),
            _resident((D, IN_WIDTH)),
            _resident((1, 2 * D_MODEL)),
            _resident((CONV_WIDTH, REC_WIDTH)),
            _resident((1, REC_WIDTH)),
            _resident((GATE_TILES, GATE_TILE, GATE_TILE)),
            _resident((1, REC_WIDTH)),
            _resident((GATE_TILES, GATE_TILE, GATE_TILE)),
            _resident((1, REC_WIDTH)),
            _resident((1, REC_WIDTH)),
        ],
        out_specs=(row(QK_WIDTH), row(QK_WIDTH), row(ATTN_WIDTH), row(REC_WIDTH), row(2 * D_MODEL)),
        out_shape=out_shape,
        scratch_shapes=[
            pltpu.VMEM((V7X_SUBLANES, REC_WIDTH), F32),
            pltpu.VMEM((1, REC_WIDTH), F32),
        ],
        compiler_params=pltpu.CompilerParams(
            dimension_semantics=("arbitrary", "arbitrary"),
            vmem_limit_bytes=48 * 1024 * 1024,
        ),
        name="in_proj_recurrent",
    )(x, w_in, b_gate, conv_w, conv_b, wa, b_a, wi, b_i, lru_lambda)


def _diff_attention_kernel(lam_init, q_ref, k_ref, v_ref, lq1_ref, lk1_ref, lq2_ref, lk2_ref, subg_ref,
                           o_ref, s0_ref, s1_ref):
    tq, tk = Q_TILE, KV_TILE
    n_chunks = tk // V7X_LANES
    head = pl.program_id(1)
    i = pl.program_id(2)

    lam = (jnp.exp(jnp.sum(lq1_ref[...] * lk1_ref[...], axis=-1, keepdims=True))
           - jnp.exp(jnp.sum(lq2_ref[...] * lk2_ref[...], axis=-1, keepdims=True))
           + lam_init)

    slope = jnp.exp2(-(head + 1).astype(F32) * jnp.ones((1, 1), F32))

    q = q_ref[...]
    lane = lax.broadcasted_iota(jnp.int32, q.shape, 1)
    zero = jnp.zeros_like(q)
    q0 = jnp.where(lane < HEAD_DIM, q, zero)
    q1 = jnp.where(lane >= HEAD_DIM, q, zero)

    col = lax.broadcasted_iota(jnp.int32, (1, tk), 1).astype(F32)
    contract_last = (((1,), (1,)), ((), ()))

    def scores(j):
        kj = k_ref[pl.ds(pl.multiple_of(j * tk, tk), tk), :]
        bias = slope * (col + (j * tk).astype(F32))
        s0 = lax.dot_general(q0, kj, contract_last, preferred_element_type=F32) + bias
        s1 = lax.dot_general(q1, kj, contract_last, preferred_element_type=F32) + bias
        return s0, s1

    def lane_chunk_reduce(op, acc, s):
        for c in range(n_chunks):
            acc = op(acc, s[:, c * V7X_LANES:(c + 1) * V7X_LANES])
        return acc

    def pass1(j, carry):
        m0, m1 = carry
        s0, s1 = scores(j)
        s0_ref[j] = s0
        s1_ref[j] = s1
        return lane_chunk_reduce(jnp.maximum, m0, s0), lane_chunk_reduce(jnp.maximum, m1, s1)

    neg_inf = jnp.full((tq, V7X_LANES), -jnp.inf, F32)
    m0, m1 = lax.fori_loop(0, i, pass1, (neg_inf, neg_inf))
    s0, s1 = scores(i)
    causal = (lax.broadcasted_iota(jnp.int32, (tq, tk), 0)
              >= lax.broadcasted_iota(jnp.int32, (tq, tk), 1))
    s0 = jnp.where(causal, s0, -jnp.inf)
    s1 = jnp.where(causal, s1, -jnp.inf)
    s0_ref[i] = s0
    s1_ref[i] = s1
    m0 = lane_chunk_reduce(jnp.maximum, m0, s0)
    m1 = lane_chunk_reduce(jnp.maximum, m1, s1)
    m0 = jnp.broadcast_to(jnp.max(m0, axis=-1, keepdims=True), (tq, tk))
    m1 = jnp.broadcast_to(jnp.max(m1, axis=-1, keepdims=True), (tq, tk))

    def pass2(j, carry):
        l0, l1 = carry
        p0 = jnp.exp(s0_ref[j] - m0)
        p1 = jnp.exp(s1_ref[j] - m1)
        s0_ref[j] = p0
        s1_ref[j] = p1
        return lane_chunk_reduce(jnp.add, l0, p0), lane_chunk_reduce(jnp.add, l1, p1)

    zeros = jnp.zeros((tq, V7X_LANES), F32)
    l0, l1 = lax.fori_loop(0, i + 1, pass2, (zeros, zeros))
    l0 = jnp.sum(l0, axis=-1, keepdims=True)
    l1 = jnp.sum(l1, axis=-1, keepdims=True)
    w0 = jnp.broadcast_to(1.0 / l0, (tq, tk))
    w1 = jnp.broadcast_to(lam / l1, (tq, tk))

    def pass3(j, acc):
        a = (s0_ref[j] * w0 - s1_ref[j] * w1).astype(BF16)
        vj = v_ref[pl.ds(pl.multiple_of(j * tk, tk), tk), :]
        return acc + _dot(a, vj)

    o = lax.fori_loop(0, i + 1, pass3, zeros)
    o = o * lax.rsqrt(jnp.mean(o * o, axis=-1, keepdims=True) + LN_EPS)
    o = o * subg_ref[...] * (1.0 - lam_init)
    o_ref[...] = o.astype(BF16)


def _diff_attention(q, k, v, lq1, lk1, lq2, lk2, subln_g, lam_init):
    B, S, _ = q.shape
    tq, tk = Q_TILE, KV_TILE
    q_spec = pl.BlockSpec((None, tq, V_DIM), lambda b, h, i: (b, i, h))
    kv_spec = pl.BlockSpec((None, S, V_DIM), lambda b, h, i: (b, 0, h))
    small = lambda n: pl.BlockSpec((1, n), lambda b, h, i: (0, 0))
    return pl.pallas_call(
        functools.partial(_diff_attention_kernel, lam_init),
        grid=(B, N_HEADS, S // tq),
        in_specs=[q_spec, kv_spec, kv_spec,
                  small(HEAD_DIM), small(HEAD_DIM), small(HEAD_DIM), small(HEAD_DIM), small(V_DIM)],
        out_specs=q_spec,
        out_shape=jax.ShapeDtypeStruct((B, S, ATTN_WIDTH), BF16),
        scratch_shapes=[
            pltpu.VMEM((S // tk, tq, tk), F32),
            pltpu.VMEM((S // tk, tq, tk), F32),
        ],
        compiler_params=pltpu.CompilerParams(
            dimension_semantics=("arbitrary", "arbitrary", "arbitrary"),
            vmem_limit_bytes=32 * 1024 * 1024,
        ),
        name="diff_attention",
    )(q, k, v, lq1, lk1, lq2, lk2, subln_g)


def _layer_norm(y, gain, bias):
    mu = jnp.mean(y, axis=-1, keepdims=True)
    yc = y - mu
    var = jnp.mean(yc * yc, axis=-1, keepdims=True)
    return yc * lax.rsqrt(var + LN_EPS) * gain + bias


def _merge_ffn_kernel(x_ref, attn_ref, rec_ref, g_ref, wba_ref, wbr_ref, wout_ref,
                      ln1g_ref, ln1b_ref, wgate_ref, wup_ref, wdown_ref, ln2g_ref, ln2b_ref,
                      o_ref):
    a = _dot(attn_ref[...], wba_ref[...])
    r = _dot(rec_ref[...], wbr_ref[...])
    merged = g_ref[:, :D_MODEL] * a + g_ref[:, D_MODEL:] * r
    mix = _dot(merged.astype(BF16), wout_ref[...])
    x1 = _layer_norm(ALPHA * x_ref[...] + mix, ln1g_ref[...], ln1b_ref[...])

    x1b = x1.astype(BF16)
    ffn = jnp.zeros_like(x1)
    for c in range(D_FF // FF_CHUNK):
        cols = slice(c * FF_CHUNK, (c + 1) * FF_CHUNK)
        gate = _dot(x1b, wgate_ref[:, cols])
        up = _dot(x1b, wup_ref[:, cols])
        hidden = (jax.nn.silu(gate) * up).astype(BF16)
        ffn = ffn + _dot(hidden, wdown_ref[cols, :])
    o_ref[...] = _layer_norm(ALPHA * x1 + ffn, ln2g_ref[...], ln2b_ref[...])


def _merge_ffn(x2d, attn2d, rec2d, g2d, wba, wbr, wout, ln1_g, ln1_b, wgate, wup, wdown, ln2_g, ln2_b):
    N, D = x2d.shape
    tm = ROW_TILE
    row = lambda width: pl.BlockSpec((tm, width), lambda i: (i, 0))
    return pl.pallas_call(
        _merge_ffn_kernel,
        grid=(N // tm,),
        in_specs=[
            row(D), row(ATTN_WIDTH), row(REC_WIDTH), row(2 * D_MODEL),
            _resident((ATTN_WIDTH, D)), _resident((REC_WIDTH, D)), _resident((D, D)),
            _resident((1, D)), _resident((1, D)),
            _resident((D, D_FF)), _resident((D, D_FF)), _resident((D_FF, D)),
            _resident((1, D)), _resident((1, D)),
        ],
        out_specs=row(D),
        out_shape=jax.ShapeDtypeStruct((N, D), F32),
        compiler_params=pltpu.CompilerParams(
            dimension_semantics=("arbitrary",),
            vmem_limit_bytes=56 * 1024 * 1024,
        ),
        name="merge_ffn",
    )(x2d, attn2d, rec2d, g2d, wba, wbr, wout, ln1_g, ln1_b, wgate, wup, wdown, ln2_g, ln2_b)


def _pack_block_diag(w):
    w = w.reshape(GATE_TILES, BLOCKS_PER_TILE, REC_BLOCK_DIM, REC_BLOCK_DIM)
    eye = jnp.eye(BLOCKS_PER_TILE, dtype=w.dtype)
    tiles = jnp.einsum("tbij,bc->tbicj", w, eye)
    return tiles.reshape(GATE_TILES, GATE_TILE, GATE_TILE)


def kernel(x, w_in, b_gate, lambda_q1, lambda_k1, lambda_q2, lambda_k2, subln_g, conv_w, conv_b,
           w_a, b_a, w_i, b_i, lru_lambda, w_br_attn, w_br_rec, w_out, ln1_g, ln1_b,
           w_gate, w_up, w_down, ln2_g, ln2_b):
    B, S, D = x.shape
    assert DEPTH == 1 and w_in.shape[0] == DEPTH
    l = 0
    lam_init = 0.8 - 0.6 * math.exp(-0.3 * l)
    row = lambda p: p[l].reshape(1, -1)

    q, k, v, rec, g = _in_proj_recurrent(
        x, w_in[l].astype(BF16), row(b_gate), conv_w[l], row(conv_b),
        _pack_block_diag(w_a[l]).astype(BF16), row(b_a),
        _pack_block_diag(w_i[l]).astype(BF16), row(b_i), row(lru_lambda))

    attn = _diff_attention(q, k, v, row(lambda_q1), row(lambda_k1), row(lambda_q2), row(lambda_k2),
                           row(subln_g), lam_init)

    N = B * S
    out = _merge_ffn(
        x.reshape(N, D), attn.reshape(N, ATTN_WIDTH), rec.reshape(N, REC_WIDTH),
        g.reshape(N, 2 * D_MODEL),
        w_br_attn[l].astype(BF16), w_br_rec[l].astype(BF16), w_out[l].astype(BF16),
        row(ln1_g), row(ln1_b),
        w_gate[l].astype(BF16), w_up[l].astype(BF16), w_down[l].astype(BF16),
        row(ln2_g), row(ln2_b))
    return out.reshape(B, S, D)
```

```python
import functools
import math

import jax
import jax.numpy as jnp
from jax import lax
from jax.experimental import pallas as pl
from jax.experimental.pallas import tpu as pltpu

D_MODEL = 1024
N_HEADS = 8
HEAD_DIM = 64
V_DIM = 2 * HEAD_DIM
QK_WIDTH = N_HEADS * 2 * HEAD_DIM
ATTN_WIDTH = N_HEADS * V_DIM
REC_WIDTH = 1024
REC_BLOCKS = 16
REC_BLOCK_DIM = REC_WIDTH // REC_BLOCKS
CONV_WIDTH = 4
RG_LRU_C = 8.0
LOG2E = math.log2(math.e)
D_FF = 2816
LN_EPS = 1e-5
DEPTH = 1
ALPHA = (2.0 * DEPTH) ** 0.25
Q_OFF = 0
K_OFF = Q_OFF + QK_WIDTH
V_OFF = K_OFF + QK_WIDTH
XR_OFF = V_OFF + ATTN_WIDTH
YR_OFF = XR_OFF + REC_WIDTH
G_OFF = YR_OFF + REC_WIDTH
IN_WIDTH = G_OFF + 2 * D_MODEL

V7X_LANES = 128
V7X_SUBLANES = 8
V7X_MXU_DIM = 256
V7X_VMEM_BYTES = 64 * 1024 * 1024

GATE_TILE = V7X_MXU_DIM
GATE_TILES = REC_WIDTH // GATE_TILE
BLOCKS_PER_TILE = GATE_TILE // REC_BLOCK_DIM

SEQ_TILE = 256
Q_TILE = 256
KV_TILE = 256
N_SCORE_BUFS = 3
ROW_TILE = 256
FF_CHUNK = 1408

BF16 = jnp.bfloat16
F32 = jnp.float32


def _dot(a, b):
    return jnp.dot(a, b, preferred_element_type=F32)


def _resident(shape):
    nd = len(shape)
    return pl.BlockSpec(shape, lambda *_: (0,) * nd, pipeline_mode=pl.Buffered(1))


def _in_proj_recurrent_kernel(x_ref, w_ref, bgate_ref, convw_ref, convb_ref, wa_ref, ba_ref,
                              wi_ref, bi_ref, lam_ref,
                              q_ref, k_ref, v_ref, rec_ref, g_ref,
                              xr_carry, h_carry):
    ts = x_ref.shape[0]

    @pl.when(pl.program_id(1) == 0)
    def _():
        xr_carry[...] = jnp.zeros_like(xr_carry)
        h_carry[...] = jnp.zeros_like(h_carry)

    xb = x_ref[...].astype(BF16)

    q_ref[...] = (_dot(xb, w_ref[:, Q_OFF:K_OFF]) * (LOG2E * HEAD_DIM ** -0.5)).astype(BF16)
    k_ref[...] = _dot(xb, w_ref[:, K_OFF:V_OFF]).astype(BF16)
    v_ref[...] = _dot(xb, w_ref[:, V_OFF:XR_OFF]).astype(BF16)

    xr = _dot(xb, w_ref[:, XR_OFF:YR_OFF])
    prev = xr_carry[...]
    xr_carry[...] = xr[ts - V7X_SUBLANES:, :]
    row8 = lax.broadcasted_iota(jnp.int32, (V7X_SUBLANES, REC_WIDTH), 0)
    xc = xr * convw_ref[CONV_WIDTH - 1:CONV_WIDTH, :] + convb_ref[...]
    for shift in range(1, CONV_WIDTH):
        rolled = pltpu.roll(xr, shift, 0)
        head = jnp.where(row8 < shift, pltpu.roll(prev, shift, 0), rolled[:V7X_SUBLANES, :])
        shifted = jnp.concatenate([head, rolled[V7X_SUBLANES:, :]], axis=0)
        xc = xc + shifted * convw_ref[CONV_WIDTH - 1 - shift:CONV_WIDTH - shift, :]

    xcb = xc.astype(BF16)
    ga, gi = [], []
    for t in range(GATE_TILES):
        cols = slice(t * GATE_TILE, (t + 1) * GATE_TILE)
        ga.append(_dot(xcb[:, cols], wa_ref[t]))
        gi.append(_dot(xcb[:, cols], wi_ref[t]))
    r = jax.nn.sigmoid(jnp.concatenate(ga, axis=1) + ba_ref[...])
    gate_i = jax.nn.sigmoid(jnp.concatenate(gi, axis=1) + bi_ref[...])

    neg_lam = -lam_ref[...]
    softplus = jnp.maximum(neg_lam, 0.0) + jnp.log1p(jnp.exp(-jnp.abs(neg_lam)))
    log_a = (-RG_LRU_C) * r * softplus
    a = jnp.exp(log_a)
    u = jnp.sqrt(-jnp.tanh(log_a) * (a * a + 1.0)) * (gate_i * xc)

    h_prev = h_carry[...]
    hs = []
    for grp in range(ts // V7X_SUBLANES):
        rows = slice(grp * V7X_SUBLANES, (grp + 1) * V7X_SUBLANES)
        ag, hg = a[rows, :], u[rows, :]
        for d in (1, 2, 4):
            keep = row8 >= d
            a_s = jnp.where(keep, pltpu.roll(ag, d, 0), 1.0)
            h_s = jnp.where(keep, pltpu.roll(hg, d, 0), 0.0)
            hg = hg + ag * h_s
            ag = ag * a_s
        hg = hg + ag * h_prev
        h_prev = hg[V7X_SUBLANES - 1:, :]
        hs.append(hg)
    h_carry[...] = h_prev
    h = jnp.concatenate(hs, axis=0)

    yr = _dot(xb, w_ref[:, YR_OFF:G_OFF])
    rec_ref[...] = (h * jax.nn.gelu(yr)).astype(BF16)

    gl = _dot(xb, w_ref[:, G_OFF:IN_WIDTH])
    g_ref[...] = jax.nn.sigmoid(gl + bgate_ref[...])


def _in_proj_recurrent(x, w_in, b_gate, conv_w, conv_b, wa, b_a, wi, b_i, lru_lambda):
    B, S, D = x.shape
    ts = SEQ_TILE
    row = lambda width: pl.BlockSpec((None, ts, width), lambda b, i: (b, i, 0))
    out_shape = (
        jax.ShapeDtypeStruct((B, S, QK_WIDTH), BF16),
        jax.ShapeDtypeStruct((B, S, QK_WIDTH), BF16),
        jax.ShapeDtypeStruct((B, S, ATTN_WIDTH), BF16),
        jax.ShapeDtypeStruct((B, S, REC_WIDTH), BF16),
        jax.ShapeDtypeStruct((B, S, 2 * D_MODEL), F32),
    )
    return pl.pallas_call(
        _in_proj_recurrent_kernel,
        grid=(B, S // ts),
        in_specs=[
            row(D),
            _resident((D, IN_WIDTH)),
            _resident((1, 2 * D_MODEL)),
            _resident((CONV_WIDTH, REC_WIDTH)),
            _resident((1, REC_WIDTH)),
            _resident((GATE_TILES, GATE_TILE, GATE_TILE)),
            _resident((1, REC_WIDTH)),
            _resident((GATE_TILES, GATE_TILE, GATE_TILE)),
            _resident((1, REC_WIDTH)),
            _resident((1, REC_WIDTH)),
        ],
        out_specs=(row(QK_WIDTH), row(QK_WIDTH), row(ATTN_WIDTH), row(REC_WIDTH), row(2 * D_MODEL)),
        out_shape=out_shape,
        scratch_shapes=[
            pltpu.VMEM((V7X_SUBLANES, REC_WIDTH), F32),
            pltpu.VMEM((1, REC_WIDTH), F32),
        ],
        compiler_params=pltpu.CompilerParams(
            dimension_semantics=("arbitrary", "arbitrary"),
            vmem_limit_bytes=48 * 1024 * 1024,
        ),
        name="in_proj_recurrent",
    )(x, w_in, b_gate, conv_w, conv_b, wa, b_a, wi, b_i, lru_lambda)


def _diff_attention_kernel(lam_init, q_ref, k_ref, v_ref, lq1_ref, lk1_ref, lq2_ref, lk2_ref, subg_ref,
                           o_ref, s0_ref, s1_ref):
    tq, tk = Q_TILE, KV_TILE
    seq = q_ref.shape[0]
    n_chunks = tk // V7X_LANES
    head = pl.program_id(1)

    lam = (jnp.exp(jnp.sum(lq1_ref[...] * lk1_ref[...], axis=-1, keepdims=True))
           - jnp.exp(jnp.sum(lq2_ref[...] * lk2_ref[...], axis=-1, keepdims=True))
           + lam_init)

    slope = LOG2E * jnp.exp2(-(head + 1).astype(F32) * jnp.ones((1, 1), F32))
    col_bias = slope * lax.broadcasted_iota(jnp.int32, (1, tk), 1).astype(F32)
    lane = lax.broadcasted_iota(jnp.int32, (tq, V_DIM), 1)
    causal = (lax.broadcasted_iota(jnp.int32, (tq, tk), 0)
              >= lax.broadcasted_iota(jnp.int32, (tq, tk), 1))
    contract_last = (((1,), (1,)), ((), ()))
    subg = subg_ref[...]

    def lane_chunk_reduce(op, acc, s):
        for c in range(n_chunks):
            chunk = s[:, c * V7X_LANES:(c + 1) * V7X_LANES]
            acc = chunk if acc is None else op(acc, chunk)
        return acc

    n_q = seq // tq
    tiles = [dict() for _ in range(n_q)]

    def pass1_tile(i, j):
        st = tiles[i]
        if j == 0:
            q = q_ref[i * tq:(i + 1) * tq, :]
            zero = jnp.zeros_like(q)
            st["q0"] = jnp.where(lane < HEAD_DIM, q, zero)
            st["q1"] = jnp.where(lane >= HEAD_DIM, q, zero)
            st["m0"] = st["m1"] = None
        kj = k_ref[j * tk:(j + 1) * tk, :]
        bias = col_bias + slope * float(j * tk)
        s0 = lax.dot_general(st["q0"], kj, contract_last, preferred_element_type=F32) + bias
        s1 = lax.dot_general(st["q1"], kj, contract_last, preferred_element_type=F32) + bias
        if j == i:
            s0 = jnp.where(causal, s0, -jnp.inf)
            s1 = jnp.where(causal, s1, -jnp.inf)
        s0_ref[i % N_SCORE_BUFS, j] = s0
        s1_ref[i % N_SCORE_BUFS, j] = s1
        st["m0"] = lane_chunk_reduce(jnp.maximum, st["m0"], s0)
        st["m1"] = lane_chunk_reduce(jnp.maximum, st["m1"], s1)

    def pass2_tile(i, j):
        st = tiles[i]
        if j == 0:
            st["m0"] = jnp.broadcast_to(jnp.max(st["m0"], axis=-1, keepdims=True), (tq, tk))
            st["m1"] = jnp.broadcast_to(jnp.max(st["m1"], axis=-1, keepdims=True), (tq, tk))
            st["l0"] = st["l1"] = None
        p0 = jnp.exp2(s0_ref[i % N_SCORE_BUFS, j] - st["m0"])
        p1 = jnp.exp2(s1_ref[i % N_SCORE_BUFS, j] - st["m1"])
        s0_ref[i % N_SCORE_BUFS, j] = p0
        s1_ref[i % N_SCORE_BUFS, j] = p1
        st["l0"] = lane_chunk_reduce(jnp.add, st["l0"], p0)
        st["l1"] = lane_chunk_reduce(jnp.add, st["l1"], p1)

    def pass3_tile(i, j):
        st = tiles[i]
        if j == 0:
            st["l0"] = jnp.sum(st["l0"], axis=-1, keepdims=True)
            l1 = jnp.sum(st["l1"], axis=-1, keepdims=True)
            st["ratio"] = jnp.broadcast_to(lam * st["l0"] / l1, (tq, tk))
            st["o"] = None
        a = (s0_ref[i % N_SCORE_BUFS, j] - s1_ref[i % N_SCORE_BUFS, j] * st["ratio"]).astype(BF16)
        pv = _dot(a, v_ref[j * tk:(j + 1) * tk, :])
        st["o"] = pv if st["o"] is None else st["o"] + pv
        if j == i:
            o = st["o"] * (1.0 / st["l0"])
            o = o * lax.rsqrt(jnp.mean(o * o, axis=-1, keepdims=True) + LN_EPS)
            o = o * subg * (1.0 - lam_init)
            o_ref[i * tq:(i + 1) * tq, :] = o.astype(BF16)
            st.clear()

    for stage in range(n_q + 2):
        for j in range(n_q):
            for phase, fn in enumerate((pass1_tile, pass2_tile, pass3_tile)):
                i = stage - phase
                if 0 <= i < n_q and j <= i:
                    fn(i, j)


def _diff_attention(q, k, v, lq1, lk1, lq2, lk2, subln_g, lam_init):
    B, S, _ = q.shape
    tq, tk = Q_TILE, KV_TILE
    head_spec = pl.BlockSpec((None, S, V_DIM), lambda b, h: (b, 0, h))
    small = lambda n: pl.BlockSpec((1, n), lambda b, h: (0, 0))
    return pl.pallas_call(
        functools.partial(_diff_attention_kernel, lam_init),
        grid=(B, N_HEADS),
        in_specs=[head_spec, head_spec, head_spec,
                  small(HEAD_DIM), small(HEAD_DIM), small(HEAD_DIM), small(HEAD_DIM), small(V_DIM)],
        out_specs=head_spec,
        out_shape=jax.ShapeDtypeStruct((B, S, ATTN_WIDTH), BF16),
        scratch_shapes=[
            pltpu.VMEM((N_SCORE_BUFS, S // tk, tq, tk), F32),
            pltpu.VMEM((N_SCORE_BUFS, S // tk, tq, tk), F32),
        ],
        compiler_params=pltpu.CompilerParams(
            dimension_semantics=("arbitrary", "arbitrary"),
            vmem_limit_bytes=32 * 1024 * 1024,
        ),
        name="diff_attention",
    )(q, k, v, lq1, lk1, lq2, lk2, subln_g)


def _layer_norm(y, gain, bias):
    mu = jnp.mean(y, axis=-1, keepdims=True)
    yc = y - mu
    var = jnp.mean(yc * yc, axis=-1, keepdims=True)
    return yc * lax.rsqrt(var + LN_EPS) * gain + bias


def _merge_ffn_kernel(x_ref, attn_ref, rec_ref, g_ref, wba_ref, wbr_ref, wout_ref,
                      ln1g_ref, ln1b_ref, wgate_ref, wup_ref, wdown_ref, ln2g_ref, ln2b_ref,
                      o_ref):
    a = _dot(attn_ref[...], wba_ref[...])
    r = _dot(rec_ref[...], wbr_ref[...])
    merged = g_ref[:, :D_MODEL] * a + g_ref[:, D_MODEL:] * r
    mix = _dot(merged.astype(BF16), wout_ref[...])
    x1 = _layer_norm(ALPHA * x_ref[...] + mix, ln1g_ref[...], ln1b_ref[...])

    x1b = x1.astype(BF16)
    ffn = jnp.zeros_like(x1)
    for c in range(D_FF // FF_CHUNK):
        cols = slice(c * FF_CHUNK, (c + 1) * FF_CHUNK)
        gate = _dot(x1b, wgate_ref[:, cols])
        up = _dot(x1b, wup_ref[:, cols])
        hidden = (jax.nn.silu(gate) * up).astype(BF16)
        ffn = ffn + _dot(hidden, wdown_ref[cols, :])
    o_ref[...] = _layer_norm(ALPHA * x1 + ffn, ln2g_ref[...], ln2b_ref[...])


def _merge_ffn(x2d, attn2d, rec2d, g2d, wba, wbr, wout, ln1_g, ln1_b, wgate, wup, wdown, ln2_g, ln2_b):
    N, D = x2d.shape
    tm = ROW_TILE
    row = lambda width: pl.BlockSpec((tm, width), lambda i: (i, 0))
    return pl.pallas_call(
        _merge_ffn_kernel,
        grid=(N // tm,),
        in_specs=[
            row(D), row(ATTN_WIDTH), row(REC_WIDTH), row(2 * D_MODEL),
            _resident((ATTN_WIDTH, D)), _resident((REC_WIDTH, D)), _resident((D, D)),
            _resident((1, D)), _resident((1, D)),
            _resident((D, D_FF)), _resident((D, D_FF)), _resident((D_FF, D)),
            _resident((1, D)), _resident((1, D)),
        ],
        out_specs=row(D),
        out_shape=jax.ShapeDtypeStruct((N, D), F32),
        compiler_params=pltpu.CompilerParams(
            dimension_semantics=("arbitrary",),
            vmem_limit_bytes=56 * 1024 * 1024,
        ),
        name="merge_ffn",
    )(x2d, attn2d, rec2d, g2d, wba, wbr, wout, ln1_g, ln1_b, wgate, wup, wdown, ln2_g, ln2_b)


def _pack_block_diag(w):
    w = w.reshape(GATE_TILES, BLOCKS_PER_TILE, REC_BLOCK_DIM, REC_BLOCK_DIM)
    eye = jnp.eye(BLOCKS_PER_TILE, dtype=w.dtype)
    tiles = jnp.einsum("tbij,bc->tbicj", w, eye)
    return tiles.reshape(GATE_TILES, GATE_TILE, GATE_TILE)


def kernel(x, w_in, b_gate, lambda_q1, lambda_k1, lambda_q2, lambda_k2, subln_g, conv_w, conv_b,
           w_a, b_a, w_i, b_i, lru_lambda, w_br_attn, w_br_rec, w_out, ln1_g, ln1_b,
           w_gate, w_up, w_down, ln2_g, ln2_b):
    B, S, D = x.shape
    assert DEPTH == 1 and w_in.shape[0] == DEPTH
    l = 0
    lam_init = 0.8 - 0.6 * math.exp(-0.3 * l)
    row = lambda p: p[l].reshape(1, -1)

    q, k, v, rec, g = _in_proj_recurrent(
        x, w_in[l].astype(BF16), row(b_gate), conv_w[l], row(conv_b),
        _pack_block_diag(w_a[l]).astype(BF16), row(b_a),
        _pack_block_diag(w_i[l]).astype(BF16), row(b_i), row(lru_lambda))

    attn = _diff_attention(q, k, v, row(lambda_q1), row(lambda_k1), row(lambda_q2), row(lambda_k2),
                           row(subln_g), lam_init)

    N = B * S
    out = _merge_ffn(
        x.reshape(N, D), attn.reshape(N, ATTN_WIDTH), rec.reshape(N, REC_WIDTH),
        g.reshape(N, 2 * D_MODEL),
        w_br_attn[l].astype(BF16), w_br_rec[l].astype(BF16), w_out[l].astype(BF16),
        row(ln1_g), row(ln1_b),
        w_gate[l].astype(BF16), w_up[l].astype(BF16), w_down[l].astype(BF16),
        row(ln2_g), row(ln2_b))
    return out.reshape(B, S, D)
```

```python
import functools
import math

import jax
import jax.numpy as jnp
from jax import lax
from jax.experimental import pallas as pl
from jax.experimental.pallas import tpu as pltpu

D_MODEL = 1024
N_HEADS = 8
HEAD_DIM = 64
V_DIM = 2 * HEAD_DIM
QK_WIDTH = N_HEADS * 2 * HEAD_DIM
ATTN_WIDTH = N_HEADS * V_DIM
REC_WIDTH = 1024
REC_BLOCKS = 16
REC_BLOCK_DIM = REC_WIDTH // REC_BLOCKS
CONV_WIDTH = 4
RG_LRU_C = 8.0
LOG2E = math.log2(math.e)
D_FF = 2816
LN_EPS = 1e-5
DEPTH = 1
ALPHA = (2.0 * DEPTH) ** 0.25
Q_OFF = 0
K_OFF = Q_OFF + QK_WIDTH
V_OFF = K_OFF + QK_WIDTH
XR_OFF = V_OFF + ATTN_WIDTH
YR_OFF = XR_OFF + REC_WIDTH
G_OFF = YR_OFF + REC_WIDTH
IN_WIDTH = G_OFF + 2 * D_MODEL

V7X_LANES = 128
V7X_SUBLANES = 8
V7X_MXU_DIM = 256
V7X_VMEM_BYTES = 64 * 1024 * 1024

GATE_TILE = V7X_MXU_DIM
GATE_TILES = REC_WIDTH // GATE_TILE
BLOCKS_PER_TILE = GATE_TILE // REC_BLOCK_DIM

SEQ_TILE = 512
Q_TILE = 256
KV_TILE = 256
N_SCORE_BUFS = 3
ROW_TILE = 512
SUB_ROWS = 256
FF_CHUNKS = ((0, 1536), (1536, D_FF))

BF16 = jnp.bfloat16
F32 = jnp.float32


def _dot(a, b):
    return jnp.dot(a, b, preferred_element_type=F32)


def _resident(shape):
    nd = len(shape)
    return pl.BlockSpec(shape, lambda *_: (0,) * nd, pipeline_mode=pl.Buffered(1))


def _gelu_tanh(x):
    c = -2.0 * math.sqrt(2.0 / math.pi) * LOG2E
    neg_2y_log2 = x * (c + (0.044715 * c) * (x * x))
    return x / (1.0 + jnp.exp2(neg_2y_log2))


def _in_proj_recurrent_kernel(x_ref, w_ref, convw_ref, convb_ref, wa_ref, ba_ref,
                              wi_ref, bi_ref, lam_ref,
                              q_ref, k_ref, v_ref, rec_ref, g_ref,
                              xb_ref, xr_carry, h_carry):
    ts = x_ref.shape[0]

    @pl.when(pl.program_id(1) == 0)
    def _():
        xr_carry[...] = jnp.zeros_like(xr_carry)
        h_carry[...] = jnp.zeros_like(h_carry)

    xb_ref[...] = x_ref[...].astype(BF16)
    row8 = lax.broadcasted_iota(jnp.int32, (V7X_SUBLANES, GATE_TILE), 0)

    def project(off, width=GATE_TILE):
        return _dot(xb_ref[...], w_ref[:, off:off + width])

    def independent_projections():
        for c in range(0, QK_WIDTH, GATE_TILE):
            q_ref[:, c:c + GATE_TILE] = (project(Q_OFF + c) * (LOG2E * HEAD_DIM ** -0.5)).astype(BF16)
            yield
        for c in range(0, QK_WIDTH, GATE_TILE):
            k_ref[:, c:c + GATE_TILE] = project(K_OFF + c).astype(BF16)
            yield
        for c in range(0, ATTN_WIDTH, GATE_TILE):
            v_ref[:, c:c + GATE_TILE] = project(V_OFF + c).astype(BF16)
            yield
        for c in range(0, 2 * D_MODEL, GATE_TILE):
            g_ref[:, c:c + GATE_TILE] = project(G_OFF + c)
            yield

    def recurrent_chain(t, xr):
        cols = slice(t * GATE_TILE, (t + 1) * GATE_TILE)
        prev = xr_carry[:, cols]
        xr_carry[:, cols] = xr[ts - V7X_SUBLANES:, :]
        xc = xr * convw_ref[CONV_WIDTH - 1:CONV_WIDTH, cols] + convb_ref[:, cols]
        for shift in range(1, CONV_WIDTH):
            rolled = pltpu.roll(xr, shift, 0)
            head = jnp.where(row8 < shift, pltpu.roll(prev, shift, 0), rolled[:V7X_SUBLANES, :])
            shifted = jnp.concatenate([head, rolled[V7X_SUBLANES:, :]], axis=0)
            xc = xc + shifted * convw_ref[CONV_WIDTH - 1 - shift:CONV_WIDTH - shift, cols]
        yield

        xcb = xc.astype(BF16)
        r = jax.nn.sigmoid(_dot(xcb, wa_ref[t]) + ba_ref[:, cols])
        gate_i = jax.nn.sigmoid(_dot(xcb, wi_ref[t]) + bi_ref[:, cols])
        neg_lam = -lam_ref[:, cols]
        softplus = jnp.maximum(neg_lam, 0.0) + jnp.log1p(jnp.exp(-jnp.abs(neg_lam)))
        decay = (RG_LRU_C * softplus) * r
        a = jnp.exp2(decay * (-LOG2E))
        var = jnp.tanh(decay) * (a * a + 1.0)
        std = jnp.where(var > 0.0, var * lax.rsqrt(var), 0.0)
        u = std * (gate_i * xc)
        yield

        h_prev = h_carry[:, cols]
        hs = []
        n_groups = ts // V7X_SUBLANES
        for grp in range(n_groups):
            rows = slice(grp * V7X_SUBLANES, (grp + 1) * V7X_SUBLANES)
            ag, hg = a[rows, :], u[rows, :]
            for d in (1, 2, 4):
                keep = row8 >= d
                a_s = jnp.where(keep, pltpu.roll(ag, d, 0), 1.0)
                h_s = jnp.where(keep, pltpu.roll(hg, d, 0), 0.0)
                hg = hg + ag * h_s
                ag = ag * a_s
            hg = hg + ag * h_prev
            h_prev = hg[V7X_SUBLANES - 1:, :]
            hs.append(hg)
            if (grp + 1) % (n_groups // 2) == 0:
                yield
        h_carry[:, cols] = h_prev
        h = jnp.concatenate(hs, axis=0)

        yr = project(YR_OFF + t * GATE_TILE)
        rec_ref[:, cols] = (h * _gelu_tanh(yr)).astype(BF16)
        yield

    filler = independent_projections()
    xrs = [project(XR_OFF + t * GATE_TILE) for t in range(GATE_TILES)]
    for t in range(GATE_TILES):
        for _ in recurrent_chain(t, xrs[t]):
            next(filler, None)
    for _ in filler:
        pass


def _in_proj_recurrent(x, w_in, conv_w, conv_b, wa, b_a, wi, b_i, lru_lambda):
    B, S, D = x.shape
    ts = SEQ_TILE
    row = lambda width: pl.BlockSpec((None, ts, width), lambda b, i: (b, i, 0))
    out_shape = (
        jax.ShapeDtypeStruct((B, S, QK_WIDTH), BF16),
        jax.ShapeDtypeStruct((B, S, QK_WIDTH), BF16),
        jax.ShapeDtypeStruct((B, S, ATTN_WIDTH), BF16),
        jax.ShapeDtypeStruct((B, S, REC_WIDTH), BF16),
        jax.ShapeDtypeStruct((B, S, 2 * D_MODEL), F32),
    )
    return pl.pallas_call(
        _in_proj_recurrent_kernel,
        grid=(B, S // ts),
        in_specs=[
            row(D),
            _resident((D, IN_WIDTH)),
            _resident((CONV_WIDTH, REC_WIDTH)),
            _resident((1, REC_WIDTH)),
            _resident((GATE_TILES, GATE_TILE, GATE_TILE)),
            _resident((1, REC_WIDTH)),
            _resident((GATE_TILES, GATE_TILE, GATE_TILE)),
            _resident((1, REC_WIDTH)),
            _resident((1, REC_WIDTH)),
        ],
        out_specs=(row(QK_WIDTH), row(QK_WIDTH), row(ATTN_WIDTH), row(REC_WIDTH), row(2 * D_MODEL)),
        out_shape=out_shape,
        scratch_shapes=[
            pltpu.VMEM((ts, D), BF16),
            pltpu.VMEM((V7X_SUBLANES, REC_WIDTH), F32),
            pltpu.VMEM((1, REC_WIDTH), F32),
        ],
        compiler_params=pltpu.CompilerParams(
            dimension_semantics=("arbitrary", "arbitrary"),
            vmem_limit_bytes=48 * 1024 * 1024,
        ),
        name="in_proj_recurrent",
    )(x, w_in, conv_w, conv_b, wa, b_a, wi, b_i, lru_lambda)


def _diff_attention_kernel(lam_init, q_ref, k_ref, v_ref, lq1_ref, lk1_ref, lq2_ref, lk2_ref, subg_ref,
                           o_ref, s0_ref, s1_ref):
    tq, tk = Q_TILE, KV_TILE
    seq = q_ref.shape[0]
    n_chunks = tk // V7X_LANES
    head = pl.program_id(1)

    lam = (jnp.exp(jnp.sum(lq1_ref[...] * lk1_ref[...], axis=-1, keepdims=True))
           - jnp.exp(jnp.sum(lq2_ref[...] * lk2_ref[...], axis=-1, keepdims=True))
           + lam_init)

    slope = LOG2E * jnp.exp2(-(head + 1).astype(F32) * jnp.ones((1, 1), F32))
    col_bias = slope * lax.broadcasted_iota(jnp.int32, (1, tk), 1).astype(F32)
    lane = lax.broadcasted_iota(jnp.int32, (tq, V_DIM), 1)
    causal = (lax.broadcasted_iota(jnp.int32, (tq, tk), 0)
              >= lax.broadcasted_iota(jnp.int32, (tq, tk), 1))
    contract_last = (((1,), (1,)), ((), ()))
    subg = subg_ref[...]

    def lane_chunk_reduce(op, acc, s):
        for c in range(n_chunks):
            chunk = s[:, c * V7X_LANES:(c + 1) * V7X_LANES]
            acc = chunk if acc is None else op(acc, chunk)
        return acc

    n_q = seq // tq
    tiles = [dict() for _ in range(n_q)]

    def pass1_tile(i, j):
        st = tiles[i]
        if j == 0:
            q = q_ref[i * tq:(i + 1) * tq, :]
            zero = jnp.zeros_like(q)
            st["q0"] = jnp.where(lane < HEAD_DIM, q, zero)
            st["q1"] = jnp.where(lane >= HEAD_DIM, q, zero)
            st["m0"] = st["m1"] = None
        kj = k_ref[j * tk:(j + 1) * tk, :]
        bias = col_bias + slope * float(j * tk)
        s0 = lax.dot_general(st["q0"], kj, contract_last, preferred_element_type=F32) + bias
        s1 = lax.dot_general(st["q1"], kj, contract_last, preferred_element_type=F32) + bias
        if j == i:
            s0 = jnp.where(causal, s0, -jnp.inf)
            s1 = jnp.where(causal, s1, -jnp.inf)
        s0_ref[i % N_SCORE_BUFS, j] = s0
        s1_ref[i % N_SCORE_BUFS, j] = s1
        st["m0"] = lane_chunk_reduce(jnp.maximum, st["m0"], s0)
        st["m1"] = lane_chunk_reduce(jnp.maximum, st["m1"], s1)

    def pass2_tile(i, j):
        st = tiles[i]
        if j == 0:
            st["m0"] = jnp.broadcast_to(jnp.max(st["m0"], axis=-1, keepdims=True), (tq, tk))
            st["m1"] = jnp.broadcast_to(jnp.max(st["m1"], axis=-1, keepdims=True), (tq, tk))
            st["l0"] = st["l1"] = None
        p0 = jnp.exp2(s0_ref[i % N_SCORE_BUFS, j] - st["m0"])
        p1 = jnp.exp2(s1_ref[i % N_SCORE_BUFS, j] - st["m1"])
        s0_ref[i % N_SCORE_BUFS, j] = p0
        s1_ref[i % N_SCORE_BUFS, j] = p1
        st["l0"] = lane_chunk_reduce(jnp.add, st["l0"], p0)
        st["l1"] = lane_chunk_reduce(jnp.add, st["l1"], p1)

    def pass3_tile(i, j):
        st = tiles[i]
        if j == 0:
            st["l0"] = jnp.sum(st["l0"], axis=-1, keepdims=True)
            l1 = jnp.sum(st["l1"], axis=-1, keepdims=True)
            st["ratio"] = jnp.broadcast_to(lam * st["l0"] / l1, (tq, tk))
            st["o"] = None
        a = (s0_ref[i % N_SCORE_BUFS, j] - s1_ref[i % N_SCORE_BUFS, j] * st["ratio"]).astype(BF16)
        pv = _dot(a, v_ref[j * tk:(j + 1) * tk, :])
        st["o"] = pv if st["o"] is None else st["o"] + pv
        if j == i:
            o = st["o"] * (1.0 / st["l0"])
            o = o * lax.rsqrt(jnp.mean(o * o, axis=-1, keepdims=True) + LN_EPS)
            o = o * subg * (1.0 - lam_init)
            o_ref[i * tq:(i + 1) * tq, :] = o.astype(BF16)
            st.clear()

    for stage in range(n_q + 2):
        for j in range(n_q):
            for phase, fn in enumerate((pass1_tile, pass2_tile, pass3_tile)):
                i = stage - phase
                if 0 <= i < n_q and j <= i:
                    fn(i, j)


def _diff_attention(q, k, v, lq1, lk1, lq2, lk2, subln_g, lam_init):
    B, S, _ = q.shape
    tq, tk = Q_TILE, KV_TILE
    head_spec = pl.BlockSpec((None, S, V_DIM), lambda b, h: (b, 0, h))
    small = lambda n: pl.BlockSpec((1, n), lambda b, h: (0, 0))
    return pl.pallas_call(
        functools.partial(_diff_attention_kernel, lam_init),
        grid=(B, N_HEADS),
        in_specs=[head_spec, head_spec, head_spec,
                  small(HEAD_DIM), small(HEAD_DIM), small(HEAD_DIM), small(HEAD_DIM), small(V_DIM)],
        out_specs=head_spec,
        out_shape=jax.ShapeDtypeStruct((B, S, ATTN_WIDTH), BF16),
        scratch_shapes=[
            pltpu.VMEM((N_SCORE_BUFS, S // tk, tq, tk), F32),
            pltpu.VMEM((N_SCORE_BUFS, S // tk, tq, tk), F32),
        ],
        compiler_params=pltpu.CompilerParams(
            dimension_semantics=("arbitrary", "arbitrary"),
            vmem_limit_bytes=32 * 1024 * 1024,
        ),
        name="diff_attention",
    )(q, k, v, lq1, lk1, lq2, lk2, subln_g)


def _layer_norm(y, gain, bias):
    mu = jnp.mean(y, axis=-1, keepdims=True)
    yc = y - mu
    var = jnp.mean(yc * yc, axis=-1, keepdims=True)
    return yc * lax.rsqrt(var + LN_EPS) * gain + bias


def _merge_ffn_kernel(x_ref, attn_ref, rec_ref, gl_ref, bgate_ref, wba_ref, wbr_ref, wout_ref,
                      ln1g_ref, ln1b_ref, wgate_ref, wup_ref, wdown_ref, ln2g_ref, ln2b_ref,
                      o_ref, x1b_ref):
    def sub_tile(rows):
        a = _dot(attn_ref[rows, :], wba_ref[...])
        r = _dot(rec_ref[rows, :], wbr_ref[...])
        yield
        g = jax.nn.sigmoid(gl_ref[rows, :] + bgate_ref[...])
        merged = (g[:, :D_MODEL] * a + g[:, D_MODEL:] * r).astype(BF16)
        yield
        mix = _dot(merged, wout_ref[...])
        yield
        x1 = _layer_norm(ALPHA * x_ref[rows, :] + mix, ln1g_ref[...], ln1b_ref[...])
        x1b_ref[rows, :] = x1.astype(BF16)
        yield
        ffn = None
        for lo, hi in FF_CHUNKS:
            gate = _dot(x1b_ref[rows, :], wgate_ref[:, lo:hi])
            up = _dot(x1b_ref[rows, :], wup_ref[:, lo:hi])
            yield
            hidden = (jax.nn.silu(gate) * up).astype(BF16)
            yield
            down = _dot(hidden, wdown_ref[lo:hi, :])
            ffn = down if ffn is None else ffn + down
        yield
        o_ref[rows, :] = _layer_norm(ALPHA * x1 + ffn, ln2g_ref[...], ln2b_ref[...])
        yield

    tm = x_ref.shape[0]
    slabs = [sub_tile(slice(r0, r0 + SUB_ROWS)) for r0 in range(0, tm, SUB_ROWS)]
    live = list(range(len(slabs)))
    step = 0
    while live:
        for idx in list(live):
            if step >= idx and next(slabs[idx], "done") == "done":
                live.remove(idx)
        step += 1


def _merge_ffn(x2d, attn2d, rec2d, gl2d, b_gate, wba, wbr, wout, ln1_g, ln1_b, wgate, wup, wdown, ln2_g, ln2_b):
    N, D = x2d.shape
    tm = ROW_TILE
    row = lambda width: pl.BlockSpec((tm, width), lambda i: (i, 0))
    return pl.pallas_call(
        _merge_ffn_kernel,
        grid=(N // tm,),
        in_specs=[
            row(D), row(ATTN_WIDTH), row(REC_WIDTH), row(2 * D_MODEL), _resident((1, 2 * D_MODEL)),
            _resident((ATTN_WIDTH, D)), _resident((REC_WIDTH, D)), _resident((D, D)),
            _resident((1, D)), _resident((1, D)),
            _resident((D, D_FF)), _resident((D, D_FF)), _resident((D_FF, D)),
            _resident((1, D)), _resident((1, D)),
        ],
        out_specs=row(D),
        out_shape=jax.ShapeDtypeStruct((N, D), F32),
        scratch_shapes=[pltpu.VMEM((tm, D), BF16)],
        compiler_params=pltpu.CompilerParams(
            dimension_semantics=("arbitrary",),
            vmem_limit_bytes=60 * 1024 * 1024,
        ),
        name="merge_ffn",
    )(x2d, attn2d, rec2d, gl2d, b_gate, wba, wbr, wout, ln1_g, ln1_b, wgate, wup, wdown, ln2_g, ln2_b)


def _pack_block_diag(w):
    w = w.reshape(GATE_TILES, BLOCKS_PER_TILE, REC_BLOCK_DIM, REC_BLOCK_DIM)
    eye = jnp.eye(BLOCKS_PER_TILE, dtype=w.dtype)
    tiles = jnp.einsum("tbij,bc->tbicj", w, eye)
    return tiles.reshape(GATE_TILES, GATE_TILE, GATE_TILE)


def kernel(x, w_in, b_gate, lambda_q1, lambda_k1, lambda_q2, lambda_k2, subln_g, conv_w, conv_b,
           w_a, b_a, w_i, b_i, lru_lambda, w_br_attn, w_br_rec, w_out, ln1_g, ln1_b,
           w_gate, w_up, w_down, ln2_g, ln2_b):
    B, S, D = x.shape
    assert DEPTH == 1 and w_in.shape[0] == DEPTH
    l = 0
    lam_init = 0.8 - 0.6 * math.exp(-0.3 * l)
    row = lambda p: p[l].reshape(1, -1)

    q, k, v, rec, gate_logits = _in_proj_recurrent(
        x, w_in[l].astype(BF16), conv_w[l], row(conv_b),
        _pack_block_diag(w_a[l]).astype(BF16), row(b_a),
        _pack_block_diag(w_i[l]).astype(BF16), row(b_i), row(lru_lambda))

    attn = _diff_attention(q, k, v, row(lambda_q1), row(lambda_k1), row(lambda_q2), row(lambda_k2),
                           row(subln_g), lam_init)

    N = B * S
    out = _merge_ffn(
        x.reshape(N, D), attn.reshape(N, ATTN_WIDTH), rec.reshape(N, REC_WIDTH),
        gate_logits.reshape(N, 2 * D_MODEL), row(b_gate),
        w_br_attn[l].astype(BF16), w_br_rec[l].astype(BF16), w_out[l].astype(BF16),
        row(ln1_g), row(ln1_b),
        w_gate[l].astype(BF16), w_up[l].astype(BF16), w_down[l].astype(BF16),
        row(ln2_g), row(ln2_b))
    return out.reshape(B, S, D)
```

```python
import functools
import math

import jax
import jax.numpy as jnp
from jax import lax
from jax.experimental import pallas as pl
from jax.experimental.pallas import tpu as pltpu

D_MODEL = 1024
N_HEADS = 8
HEAD_DIM = 64
V_DIM = 2 * HEAD_DIM
QK_WIDTH = N_HEADS * 2 * HEAD_DIM
ATTN_WIDTH = N_HEADS * V_DIM
REC_WIDTH = 1024
REC_BLOCKS = 16
REC_BLOCK_DIM = REC_WIDTH // REC_BLOCKS
CONV_WIDTH = 4
RG_LRU_C = 8.0
LOG2E = math.log2(math.e)
D_FF = 2816
LN_EPS = 1e-5
DEPTH = 1
ALPHA = (2.0 * DEPTH) ** 0.25
Q_OFF = 0
K_OFF = Q_OFF + QK_WIDTH
V_OFF = K_OFF + QK_WIDTH
XR_OFF = V_OFF + ATTN_WIDTH
YR_OFF = XR_OFF + REC_WIDTH
G_OFF = YR_OFF + REC_WIDTH
IN_WIDTH = G_OFF + 2 * D_MODEL

V7X_LANES = 128
V7X_SUBLANES = 8
V7X_MXU_DIM = 256
V7X_VMEM_BYTES = 64 * 1024 * 1024

GATE_TILE = V7X_MXU_DIM
GATE_TILES = REC_WIDTH // GATE_TILE
BLOCKS_PER_TILE = GATE_TILE // REC_BLOCK_DIM

SEQ_TILE = 512
Q_TILE = 256
KV_TILE = 256
N_SCORE_BUFS = 3
N_SLOPE_PARTS = 3
KPOS_SPLIT = 16
ROW_TILE = 512
SUB_ROWS = 256
FF_CHUNKS = ((0, 1536), (1536, D_FF))

BF16 = jnp.bfloat16
F32 = jnp.float32


def _dot(a, b):
    return jnp.dot(a, b, preferred_element_type=F32)


def _resident(shape):
    nd = len(shape)
    return pl.BlockSpec(shape, lambda *_: (0,) * nd, pipeline_mode=pl.Buffered(1))


def _gelu_tanh(x):
    c = -2.0 * math.sqrt(2.0 / math.pi) * LOG2E
    neg_2y_log2 = x * (c + (0.044715 * c) * (x * x))
    return x / (1.0 + jnp.exp2(neg_2y_log2))


def _in_proj_recurrent_kernel(x_ref, w_ref, convw_ref, convb_ref, wa_ref, ba_ref,
                              wi_ref, bi_ref, lam_ref,
                              q_ref, k_ref, vt_ref, rec_ref, g_ref,
                              xb_ref, xr_carry, h_carry):
    ts = x_ref.shape[0]

    @pl.when(pl.program_id(1) == 0)
    def _():
        xr_carry[...] = jnp.zeros_like(xr_carry)
        h_carry[...] = jnp.zeros_like(h_carry)

    xb_ref[...] = x_ref[...].astype(BF16)
    row8 = lax.broadcasted_iota(jnp.int32, (V7X_SUBLANES, GATE_TILE), 0)

    def project(off, width=GATE_TILE):
        return _dot(xb_ref[...], w_ref[:, off:off + width])

    def independent_projections():
        for c in range(0, QK_WIDTH, GATE_TILE):
            q_ref[:, c:c + GATE_TILE] = (project(Q_OFF + c) * (LOG2E * HEAD_DIM ** -0.5)).astype(BF16)
            yield
        for c in range(0, QK_WIDTH, GATE_TILE):
            k_ref[:, c:c + GATE_TILE] = project(K_OFF + c).astype(BF16)
            yield
        for c in range(0, ATTN_WIDTH, GATE_TILE):
            vt_ref[c:c + GATE_TILE, :] = project(V_OFF + c).T.astype(BF16)
            yield
        for c in range(0, 2 * D_MODEL, GATE_TILE):
            g_ref[:, c:c + GATE_TILE] = project(G_OFF + c)
            yield

    def recurrent_chain(t, xr):
        cols = slice(t * GATE_TILE, (t + 1) * GATE_TILE)
        prev = xr_carry[:, cols]
        xr_carry[:, cols] = xr[ts - V7X_SUBLANES:, :]
        xc = xr * convw_ref[CONV_WIDTH - 1:CONV_WIDTH, cols] + convb_ref[:, cols]
        for shift in range(1, CONV_WIDTH):
            rolled = pltpu.roll(xr, shift, 0)
            head = jnp.where(row8 < shift, pltpu.roll(prev, shift, 0), rolled[:V7X_SUBLANES, :])
            shifted = jnp.concatenate([head, rolled[V7X_SUBLANES:, :]], axis=0)
            xc = xc + shifted * convw_ref[CONV_WIDTH - 1 - shift:CONV_WIDTH - shift, cols]
        yield

        xcb = xc.astype(BF16)
        r = jax.nn.sigmoid(_dot(xcb, wa_ref[t]) + ba_ref[:, cols])
        gate_i = jax.nn.sigmoid(_dot(xcb, wi_ref[t]) + bi_ref[:, cols])
        neg_lam = -lam_ref[:, cols]
        softplus = jnp.maximum(neg_lam, 0.0) + jnp.log1p(jnp.exp(-jnp.abs(neg_lam)))
        decay = (RG_LRU_C * softplus) * r
        a = jnp.exp2(decay * (-LOG2E))
        var = jnp.tanh(decay) * (a * a + 1.0)
        std = jnp.where(var > 0.0, var * lax.rsqrt(var), 0.0)
        u = std * (gate_i * xc)
        yield

        h_prev = h_carry[:, cols]
        hs = []
        n_groups = ts // V7X_SUBLANES
        for grp in range(n_groups):
            rows = slice(grp * V7X_SUBLANES, (grp + 1) * V7X_SUBLANES)
            ag, hg = a[rows, :], u[rows, :]
            for d in (1, 2, 4):
                keep = row8 >= d
                a_s = jnp.where(keep, pltpu.roll(ag, d, 0), 1.0)
                h_s = jnp.where(keep, pltpu.roll(hg, d, 0), 0.0)
                hg = hg + ag * h_s
                ag = ag * a_s
            hg = hg + ag * h_prev
            h_prev = hg[V7X_SUBLANES - 1:, :]
            hs.append(hg)
            if (grp + 1) % (n_groups // 2) == 0:
                yield
        h_carry[:, cols] = h_prev
        h = jnp.concatenate(hs, axis=0)

        yr = project(YR_OFF + t * GATE_TILE)
        rec_ref[:, cols] = (h * _gelu_tanh(yr)).astype(BF16)
        yield

    filler = independent_projections()
    xrs = [project(XR_OFF + t * GATE_TILE) for t in range(GATE_TILES)]
    for t in range(GATE_TILES):
        for _ in recurrent_chain(t, xrs[t]):
            next(filler, None)
    for _ in filler:
        pass


def _in_proj_recurrent(x, w_in, conv_w, conv_b, wa, b_a, wi, b_i, lru_lambda):
    B, S, D = x.shape
    ts = SEQ_TILE
    row = lambda width: pl.BlockSpec((None, ts, width), lambda b, i: (b, i, 0))
    out_shape = (
        jax.ShapeDtypeStruct((B, S, QK_WIDTH), BF16),
        jax.ShapeDtypeStruct((B, S, QK_WIDTH), BF16),
        jax.ShapeDtypeStruct((B, ATTN_WIDTH, S), BF16),
        jax.ShapeDtypeStruct((B, S, REC_WIDTH), BF16),
        jax.ShapeDtypeStruct((B, S, 2 * D_MODEL), F32),
    )
    return pl.pallas_call(
        _in_proj_recurrent_kernel,
        grid=(B, S // ts),
        in_specs=[
            row(D),
            _resident((D, IN_WIDTH)),
            _resident((CONV_WIDTH, REC_WIDTH)),
            _resident((1, REC_WIDTH)),
            _resident((GATE_TILES, GATE_TILE, GATE_TILE)),
            _resident((1, REC_WIDTH)),
            _resident((GATE_TILES, GATE_TILE, GATE_TILE)),
            _resident((1, REC_WIDTH)),
            _resident((1, REC_WIDTH)),
        ],
        out_specs=(row(QK_WIDTH), row(QK_WIDTH),
                   pl.BlockSpec((None, ATTN_WIDTH, ts), lambda b, i: (b, 0, i)),
                   row(REC_WIDTH), row(2 * D_MODEL)),
        out_shape=out_shape,
        scratch_shapes=[
            pltpu.VMEM((ts, D), BF16),
            pltpu.VMEM((V7X_SUBLANES, REC_WIDTH), F32),
            pltpu.VMEM((1, REC_WIDTH), F32),
        ],
        compiler_params=pltpu.CompilerParams(
            dimension_semantics=("arbitrary", "arbitrary"),
            vmem_limit_bytes=48 * 1024 * 1024,
        ),
        name="in_proj_recurrent",
    )(x, w_in, conv_w, conv_b, wa, b_a, wi, b_i, lru_lambda)


def _diff_attention_kernel(lam_init, q_ref, k_ref, vt_ref, lq1_ref, lk1_ref, lq2_ref, lk2_ref, subg_ref,
                           o_ref, kpos_ref, ka0_ref, ka1_ref, s0_ref, s1_ref, p0_ref, p1_ref):
    tq, tk = Q_TILE, KV_TILE
    seq = q_ref.shape[0]
    head = pl.program_id(1)

    @pl.when((pl.program_id(0) == 0) & (head == 0))
    def _():
        kpos = lax.broadcasted_iota(jnp.int32, (seq, V_DIM), 0)
        local = lax.broadcasted_iota(jnp.int32, (seq, V_DIM), 1) & (HEAD_DIM - 1)
        fine_bits = kpos & (KPOS_SPLIT - 1)
        coarse = (kpos - fine_bits).astype(F32)
        fine = fine_bits.astype(F32)
        parts = jnp.where(local < N_SLOPE_PARTS, coarse,
                          jnp.where(local < 2 * N_SLOPE_PARTS, fine, 0.0))
        kpos_ref[...] = parts.astype(BF16)

    lam = (jnp.exp(jnp.sum(lq1_ref[...] * lk1_ref[...], axis=-1, keepdims=True))
           - jnp.exp(jnp.sum(lq2_ref[...] * lk2_ref[...], axis=-1, keepdims=True))
           + lam_init)

    slope = LOG2E * jnp.exp2(-(head + 1).astype(F32) * jnp.ones((1, V_DIM), F32))
    local_row = lax.broadcasted_iota(jnp.int32, (1, V_DIM), 1) & (HEAD_DIM - 1)
    slope_parts = jnp.zeros((1, V_DIM), F32)
    rest = slope
    for part in range(N_SLOPE_PARTS):
        piece = rest.astype(BF16).astype(F32)
        pairs_with = (local_row == part) | (local_row == part + N_SLOPE_PARTS)
        slope_parts = jnp.where(pairs_with, piece, slope_parts)
        rest = rest - piece
    slope_parts = slope_parts.astype(BF16)

    lane = lax.broadcasted_iota(jnp.int32, (tq, V_DIM), 1)
    lane_k = lax.broadcasted_iota(jnp.int32, (seq, V_DIM), 1)
    k_all = k_ref[...]
    ka0_ref[...] = jnp.where(lane_k < HEAD_DIM, k_all, kpos_ref[...])
    ka1_ref[...] = jnp.where(lane_k >= HEAD_DIM, k_all, kpos_ref[...])
    slope_rows = jnp.broadcast_to(slope_parts, (tq, V_DIM))
    causal = (lax.broadcasted_iota(jnp.int32, (tk, tq), 0)
              <= lax.broadcasted_iota(jnp.int32, (tk, tq), 1))
    contract_last = (((1,), (1,)), ((), ()))
    subg = subg_ref[...]
    pack_rows = 2 * V7X_SUBLANES

    def row_group_reduce(op, acc, s):
        for g in range(s.shape[0] // V7X_SUBLANES):
            chunk = s[g * V7X_SUBLANES:(g + 1) * V7X_SUBLANES, :]
            acc = chunk if acc is None else op(acc, chunk)
        return acc

    n_q = seq // tq
    tiles = [dict() for _ in range(n_q)]

    def pass1_tile(i, j):
        st = tiles[i]
        if j == 0:
            q = q_ref[i * tq:(i + 1) * tq, :]
            st["q0"] = jnp.where(lane < HEAD_DIM, q, slope_rows)
            st["q1"] = jnp.where(lane >= HEAD_DIM, q, slope_rows)
            st["m0"] = st["m1"] = None
        kv = slice(j * tk, (j + 1) * tk)
        s0 = lax.dot_general(ka0_ref[kv, :], st["q0"], contract_last, preferred_element_type=F32)
        s1 = lax.dot_general(ka1_ref[kv, :], st["q1"], contract_last, preferred_element_type=F32)
        if j == i:
            s0 = jnp.where(causal, s0, -jnp.inf)
            s1 = jnp.where(causal, s1, -jnp.inf)
        s0_ref[i % N_SCORE_BUFS, j] = s0
        s1_ref[i % N_SCORE_BUFS, j] = s1
        st["m0"] = row_group_reduce(jnp.maximum, st["m0"], s0)
        st["m1"] = row_group_reduce(jnp.maximum, st["m1"], s1)

    def pass2_tile(i, j):
        st = tiles[i]
        buf = i % N_SCORE_BUFS
        if j == 0:
            for m in ("m0", "m1"):
                st[m] = jnp.broadcast_to(jnp.max(st[m], axis=0, keepdims=True), (pack_rows, tq))
            st["l0"] = st["l1"] = None
        for g in range(tk // pack_rows):
            rows = slice(g * pack_rows, (g + 1) * pack_rows)
            p0 = jnp.exp2(s0_ref[buf, j, rows, :] - st["m0"])
            p1 = jnp.exp2(s1_ref[buf, j, rows, :] - st["m1"])
            p0_ref[buf, j, rows, :] = p0.astype(BF16)
            p1_ref[buf, j, rows, :] = p1.astype(BF16)
            st["l0"] = row_group_reduce(jnp.add, st["l0"], p0)
            st["l1"] = row_group_reduce(jnp.add, st["l1"], p1)

    def pass3_tile(i, j):
        st = tiles[i]
        buf = i % N_SCORE_BUFS
        if j == 0:
            st["l0"] = jnp.sum(st["l0"], axis=0, keepdims=True)
            l1 = jnp.sum(st["l1"], axis=0, keepdims=True)
            st["ratio"] = jnp.broadcast_to((lam * st["l0"] / l1).astype(BF16), (pack_rows, tq))
            st["o"] = None
        a = jnp.concatenate(
            [p0_ref[buf, j, g * pack_rows:(g + 1) * pack_rows, :]
             - p1_ref[buf, j, g * pack_rows:(g + 1) * pack_rows, :] * st["ratio"]
             for g in range(tk // pack_rows)], axis=0)
        pv = _dot(vt_ref[:, j * tk:(j + 1) * tk], a)
        st["o"] = pv if st["o"] is None else st["o"] + pv
        if j == i:
            o = st["o"] * (1.0 / st["l0"])
            o = o * lax.rsqrt(jnp.mean(o * o, axis=0, keepdims=True) + LN_EPS)
            o = o * subg * (1.0 - lam_init)
            o_ref[i * tq:(i + 1) * tq, :] = o.T.astype(BF16)
            st.clear()

    for stage in range(n_q + 2):
        for j in range(n_q):
            for phase, fn in enumerate((pass1_tile, pass2_tile, pass3_tile)):
                i = stage - phase
                if 0 <= i < n_q and j <= i:
                    fn(i, j)


def _diff_attention(q, k, v_t, lq1, lk1, lq2, lk2, subln_g, lam_init):
    B, S, _ = q.shape
    tq, tk = Q_TILE, KV_TILE
    head_spec = pl.BlockSpec((None, S, V_DIM), lambda b, h: (b, 0, h))
    small = lambda n: pl.BlockSpec((1, n), lambda b, h: (0, 0))
    tile_stack = (N_SCORE_BUFS, S // tk, tk, tq)
    return pl.pallas_call(
        functools.partial(_diff_attention_kernel, lam_init),
        grid=(B, N_HEADS),
        in_specs=[head_spec, head_spec, pl.BlockSpec((None, V_DIM, S), lambda b, h: (b, h, 0)),
                  small(HEAD_DIM), small(HEAD_DIM), small(HEAD_DIM), small(HEAD_DIM),
                  pl.BlockSpec((V_DIM, 1), lambda b, h: (0, 0))],
        out_specs=head_spec,
        out_shape=jax.ShapeDtypeStruct((B, S, ATTN_WIDTH), BF16),
        scratch_shapes=[
            pltpu.VMEM((S, V_DIM), BF16),
            pltpu.VMEM((S, V_DIM), BF16),
            pltpu.VMEM((S, V_DIM), BF16),
            pltpu.VMEM(tile_stack, F32),
            pltpu.VMEM(tile_stack, F32),
            pltpu.VMEM(tile_stack, BF16),
            pltpu.VMEM(tile_stack, BF16),
        ],
        compiler_params=pltpu.CompilerParams(
            dimension_semantics=("arbitrary", "arbitrary"),
            vmem_limit_bytes=32 * 1024 * 1024,
        ),
        name="diff_attention",
    )(q, k, v_t, lq1, lk1, lq2, lk2, subln_g)


def _layer_norm(y, gain, bias):
    mu = jnp.mean(y, axis=-1, keepdims=True)
    yc = y - mu
    var = jnp.mean(yc * yc, axis=-1, keepdims=True)
    return yc * lax.rsqrt(var + LN_EPS) * gain + bias


def _merge_ffn_kernel(x_ref, attn_ref, rec_ref, gl_ref, bgate_ref, wba_ref, wbr_ref, wout_ref,
                      ln1g_ref, ln1b_ref, wgate_ref, wup_ref, wdown_ref, ln2g_ref, ln2b_ref,
                      o_ref, x1b_ref):
    def sub_tile(rows):
        a = _dot(attn_ref[rows, :], wba_ref[...])
        r = _dot(rec_ref[rows, :], wbr_ref[...])
        yield
        g = jax.nn.sigmoid(gl_ref[rows, :] + bgate_ref[...])
        merged = (g[:, :D_MODEL] * a + g[:, D_MODEL:] * r).astype(BF16)
        yield
        mix = _dot(merged, wout_ref[...])
        yield
        x1 = _layer_norm(ALPHA * x_ref[rows, :] + mix, ln1g_ref[...], ln1b_ref[...])
        x1b_ref[rows, :] = x1.astype(BF16)
        yield
        ffn = None
        for lo, hi in FF_CHUNKS:
            gate = _dot(x1b_ref[rows, :], wgate_ref[:, lo:hi])
            up = _dot(x1b_ref[rows, :], wup_ref[:, lo:hi])
            yield
            hidden = (jax.nn.silu(gate) * up).astype(BF16)
            yield
            down = _dot(hidden, wdown_ref[lo:hi, :])
            ffn = down if ffn is None else ffn + down
        yield
        o_ref[rows, :] = _layer_norm(ALPHA * x1 + ffn, ln2g_ref[...], ln2b_ref[...])
        yield

    tm = x_ref.shape[0]
    slabs = [sub_tile(slice(r0, r0 + SUB_ROWS)) for r0 in range(0, tm, SUB_ROWS)]
    live = list(range(len(slabs)))
    step = 0
    while live:
        for idx in list(live):
            if step >= idx and next(slabs[idx], "done") == "done":
                live.remove(idx)
        step += 1


def _merge_ffn(x2d, attn2d, rec2d, gl2d, b_gate, wba, wbr, wout, ln1_g, ln1_b, wgate, wup, wdown, ln2_g, ln2_b):
    N, D = x2d.shape
    tm = ROW_TILE
    row = lambda width: pl.BlockSpec((tm, width), lambda i: (i, 0))
    return pl.pallas_call(
        _merge_ffn_kernel,
        grid=(N // tm,),
        in_specs=[
            row(D), row(ATTN_WIDTH), row(REC_WIDTH), row(2 * D_MODEL), _resident((1, 2 * D_MODEL)),
            _resident((ATTN_WIDTH, D)), _resident((REC_WIDTH, D)), _resident((D, D)),
            _resident((1, D)), _resident((1, D)),
            _resident((D, D_FF)), _resident((D, D_FF)), _resident((D_FF, D)),
            _resident((1, D)), _resident((1, D)),
        ],
        out_specs=row(D),
        out_shape=jax.ShapeDtypeStruct((N, D), F32),
        scratch_shapes=[pltpu.VMEM((tm, D), BF16)],
        compiler_params=pltpu.CompilerParams(
            dimension_semantics=("arbitrary",),
            vmem_limit_bytes=60 * 1024 * 1024,
        ),
        name="merge_ffn",
    )(x2d, attn2d, rec2d, gl2d, b_gate, wba, wbr, wout, ln1_g, ln1_b, wgate, wup, wdown, ln2_g, ln2_b)


def _pack_block_diag(w):
    w = w.reshape(GATE_TILES, BLOCKS_PER_TILE, REC_BLOCK_DIM, REC_BLOCK_DIM)
    eye = jnp.eye(BLOCKS_PER_TILE, dtype=w.dtype)
    tiles = jnp.einsum("tbij,bc->tbicj", w, eye)
    return tiles.reshape(GATE_TILES, GATE_TILE, GATE_TILE)


def kernel(x, w_in, b_gate, lambda_q1, lambda_k1, lambda_q2, lambda_k2, subln_g, conv_w, conv_b,
           w_a, b_a, w_i, b_i, lru_lambda, w_br_attn, w_br_rec, w_out, ln1_g, ln1_b,
           w_gate, w_up, w_down, ln2_g, ln2_b):
    B, S, D = x.shape
    assert DEPTH == 1 and w_in.shape[0] == DEPTH
    l = 0
    lam_init = 0.8 - 0.6 * math.exp(-0.3 * l)
    row = lambda p: p[l].reshape(1, -1)

    q, k, v_t, rec, gate_logits = _in_proj_recurrent(
        x, w_in[l].astype(BF16), conv_w[l], row(conv_b),
        _pack_block_diag(w_a[l]).astype(BF16), row(b_a),
        _pack_block_diag(w_i[l]).astype(BF16), row(b_i), row(lru_lambda))

    attn = _diff_attention(q, k, v_t, row(lambda_q1), row(lambda_k1), row(lambda_q2), row(lambda_k2),
                           subln_g[l].reshape(V_DIM, 1), lam_init)

    N = B * S
    out = _merge_ffn(
        x.reshape(N, D), attn.reshape(N, ATTN_WIDTH), rec.reshape(N, REC_WIDTH),
        gate_logits.reshape(N, 2 * D_MODEL), row(b_gate),
        w_br_attn[l].astype(BF16), w_br_rec[l].astype(BF16), w_out[l].astype(BF16),
        row(ln1_g), row(ln1_b),
        w_gate[l].astype(BF16), w_up[l].astype(BF16), w_down[l].astype(BF16),
        row(ln2_g), row(ln2_b))
    return out.reshape(B, S, D)
```

```python
import functools
import math

import jax
import jax.numpy as jnp
from jax import lax
from jax.experimental import pallas as pl
from jax.experimental.pallas import tpu as pltpu

D_MODEL = 1024
N_HEADS = 8
HEAD_DIM = 64
V_DIM = 2 * HEAD_DIM
QK_WIDTH = N_HEADS * 2 * HEAD_DIM
ATTN_WIDTH = N_HEADS * V_DIM
REC_WIDTH = 1024
REC_BLOCKS = 16
REC_BLOCK_DIM = REC_WIDTH // REC_BLOCKS
CONV_WIDTH = 4
RG_LRU_C = 8.0
LOG2E = math.log2(math.e)
D_FF = 2816
LN_EPS = 1e-5
DEPTH = 1
ALPHA = (2.0 * DEPTH) ** 0.25
Q_OFF = 0
K_OFF = Q_OFF + QK_WIDTH
V_OFF = K_OFF + QK_WIDTH
XR_OFF = V_OFF + ATTN_WIDTH
YR_OFF = XR_OFF + REC_WIDTH
G_OFF = YR_OFF + REC_WIDTH
IN_WIDTH = G_OFF + 2 * D_MODEL

V7X_LANES = 128
V7X_SUBLANES = 8
V7X_MXU_DIM = 256
V7X_VMEM_BYTES = 64 * 1024 * 1024

GATE_TILE = V7X_MXU_DIM
GATE_TILES = REC_WIDTH // GATE_TILE
BLOCKS_PER_TILE = GATE_TILE // REC_BLOCK_DIM

SEQ_TILE = 512
Q_TILE = 256
KV_TILE = 256
N_SCORE_BUFS = 3
V_AUG = V_DIM + 16
N_SLOPE_PARTS = 3
KPOS_SPLIT = 16
ROW_TILE = 512
SUB_ROWS = 256
FF_CHUNKS = ((0, 1536), (1536, D_FF))

BF16 = jnp.bfloat16
F32 = jnp.float32


def _dot(a, b):
    return jnp.dot(a, b, preferred_element_type=F32)


def _resident(shape):
    nd = len(shape)
    return pl.BlockSpec(shape, lambda *_: (0,) * nd, pipeline_mode=pl.Buffered(1))


def _gelu_tanh(x):
    c = -2.0 * math.sqrt(2.0 / math.pi) * LOG2E
    neg_2y_log2 = x * (c + (0.044715 * c) * (x * x))
    return x / (1.0 + jnp.exp2(neg_2y_log2))


def _in_proj_recurrent_kernel(x_ref, w_ref, convw_ref, convb_ref, wa_ref, ba_ref,
                              wi_ref, bi_ref, lam_ref,
                              q_ref, k_ref, vt_ref, rec_ref, g_ref,
                              xb_ref, xr_carry, h_carry):
    ts = x_ref.shape[0]

    @pl.when(pl.program_id(1) == 0)
    def _():
        xr_carry[...] = jnp.zeros_like(xr_carry)
        h_carry[...] = jnp.zeros_like(h_carry)

    xb_ref[...] = x_ref[...].astype(BF16)
    row8 = lax.broadcasted_iota(jnp.int32, (V7X_SUBLANES, GATE_TILE), 0)

    def project(off, width=GATE_TILE):
        return _dot(xb_ref[...], w_ref[:, off:off + width])

    def independent_projections():
        for c in range(0, QK_WIDTH, GATE_TILE):
            q_ref[:, c:c + GATE_TILE] = (project(Q_OFF + c) * (LOG2E * HEAD_DIM ** -0.5)).astype(BF16)
            yield
        for c in range(0, QK_WIDTH, GATE_TILE):
            k_ref[:, c:c + GATE_TILE] = project(K_OFF + c).astype(BF16)
            yield
        for c in range(0, ATTN_WIDTH, GATE_TILE):
            vt_ref[c:c + GATE_TILE, :] = project(V_OFF + c).T.astype(BF16)
            yield
        for c in range(0, 2 * D_MODEL, GATE_TILE):
            g_ref[:, c:c + GATE_TILE] = project(G_OFF + c)
            yield

    def recurrent_chain(t, xr):
        cols = slice(t * GATE_TILE, (t + 1) * GATE_TILE)
        prev = xr_carry[:, cols]
        xr_carry[:, cols] = xr[ts - V7X_SUBLANES:, :]
        xc = xr * convw_ref[CONV_WIDTH - 1:CONV_WIDTH, cols] + convb_ref[:, cols]
        for shift in range(1, CONV_WIDTH):
            rolled = pltpu.roll(xr, shift, 0)
            head = jnp.where(row8 < shift, pltpu.roll(prev, shift, 0), rolled[:V7X_SUBLANES, :])
            shifted = jnp.concatenate([head, rolled[V7X_SUBLANES:, :]], axis=0)
            xc = xc + shifted * convw_ref[CONV_WIDTH - 1 - shift:CONV_WIDTH - shift, cols]
        yield

        xcb = xc.astype(BF16)
        r = jax.nn.sigmoid(_dot(xcb, wa_ref[t]) + ba_ref[:, cols])
        gate_i = jax.nn.sigmoid(_dot(xcb, wi_ref[t]) + bi_ref[:, cols])
        neg_lam = -lam_ref[:, cols]
        softplus = jnp.maximum(neg_lam, 0.0) + jnp.log1p(jnp.exp(-jnp.abs(neg_lam)))
        decay = (RG_LRU_C * softplus) * r
        a = jnp.exp2(decay * (-LOG2E))
        var = jnp.tanh(decay) * (a * a + 1.0)
        std = jnp.where(var > 0.0, var * lax.rsqrt(var), 0.0)
        u = std * (gate_i * xc)
        yield

        h_prev = h_carry[:, cols]
        hs = []
        n_groups = ts // V7X_SUBLANES
        for grp in range(n_groups):
            rows = slice(grp * V7X_SUBLANES, (grp + 1) * V7X_SUBLANES)
            ag, hg = a[rows, :], u[rows, :]
            for d in (1, 2, 4):
                keep = row8 >= d
                a_s = jnp.where(keep, pltpu.roll(ag, d, 0), 1.0)
                h_s = jnp.where(keep, pltpu.roll(hg, d, 0), 0.0)
                hg = hg + ag * h_s
                ag = ag * a_s
            hg = hg + ag * h_prev
            h_prev = hg[V7X_SUBLANES - 1:, :]
            hs.append(hg)
            if (grp + 1) % (n_groups // 2) == 0:
                yield
        h_carry[:, cols] = h_prev
        h = jnp.concatenate(hs, axis=0)

        yr = project(YR_OFF + t * GATE_TILE)
        rec_ref[:, cols] = (h * _gelu_tanh(yr)).astype(BF16)
        yield

    filler = independent_projections()
    xrs = [project(XR_OFF + t * GATE_TILE) for t in range(GATE_TILES)]
    for t in range(GATE_TILES):
        for _ in recurrent_chain(t, xrs[t]):
            next(filler, None)
    for _ in filler:
        pass


def _in_proj_recurrent(x, w_in, conv_w, conv_b, wa, b_a, wi, b_i, lru_lambda):
    B, S, D = x.shape
    ts = SEQ_TILE
    row = lambda width: pl.BlockSpec((None, ts, width), lambda b, i: (b, i, 0))
    out_shape = (
        jax.ShapeDtypeStruct((B, S, QK_WIDTH), BF16),
        jax.ShapeDtypeStruct((B, S, QK_WIDTH), BF16),
        jax.ShapeDtypeStruct((B, ATTN_WIDTH, S), BF16),
        jax.ShapeDtypeStruct((B, S, REC_WIDTH), BF16),
        jax.ShapeDtypeStruct((B, S, 2 * D_MODEL), F32),
    )
    return pl.pallas_call(
        _in_proj_recurrent_kernel,
        grid=(B, S // ts),
        in_specs=[
            row(D),
            _resident((D, IN_WIDTH)),
            _resident((CONV_WIDTH, REC_WIDTH)),
            _resident((1, REC_WIDTH)),
            _resident((GATE_TILES, GATE_TILE, GATE_TILE)),
            _resident((1, REC_WIDTH)),
            _resident((GATE_TILES, GATE_TILE, GATE_TILE)),
            _resident((1, REC_WIDTH)),
            _resident((1, REC_WIDTH)),
        ],
        out_specs=(row(QK_WIDTH), row(QK_WIDTH),
                   pl.BlockSpec((None, ATTN_WIDTH, ts), lambda b, i: (b, 0, i)),
                   row(REC_WIDTH), row(2 * D_MODEL)),
        out_shape=out_shape,
        scratch_shapes=[
            pltpu.VMEM((ts, D), BF16),
            pltpu.VMEM((V7X_SUBLANES, REC_WIDTH), F32),
            pltpu.VMEM((1, REC_WIDTH), F32),
        ],
        compiler_params=pltpu.CompilerParams(
            dimension_semantics=("arbitrary", "arbitrary"),
            vmem_limit_bytes=48 * 1024 * 1024,
        ),
        name="in_proj_recurrent",
    )(x, w_in, conv_w, conv_b, wa, b_a, wi, b_i, lru_lambda)


def _diff_attention_kernel(lam_init, q_ref, k_ref, vt_ref, lq1_ref, lk1_ref, lq2_ref, lk2_ref, subg_ref,
                           o_ref, kpos_ref, ka0_ref, ka1_ref, vsum_ref, s0_ref, s1_ref, p0_ref, p1_ref):
    tq, tk = Q_TILE, KV_TILE
    seq = q_ref.shape[0]
    head = pl.program_id(1)

    @pl.when((pl.program_id(0) == 0) & (head == 0))
    def _():
        kpos = lax.broadcasted_iota(jnp.int32, (seq, V_DIM), 0)
        local = lax.broadcasted_iota(jnp.int32, (seq, V_DIM), 1) & (HEAD_DIM - 1)
        fine_bits = kpos & (KPOS_SPLIT - 1)
        coarse = (kpos - fine_bits).astype(F32)
        fine = fine_bits.astype(F32)
        parts = jnp.where(local < N_SLOPE_PARTS, coarse,
                          jnp.where(local < 2 * N_SLOPE_PARTS, fine, 0.0))
        kpos_ref[...] = parts.astype(BF16)

    lam = (jnp.exp(jnp.sum(lq1_ref[...] * lk1_ref[...], axis=-1, keepdims=True))
           - jnp.exp(jnp.sum(lq2_ref[...] * lk2_ref[...], axis=-1, keepdims=True))
           + lam_init)

    slope = LOG2E * jnp.exp2(-(head + 1).astype(F32) * jnp.ones((1, V_DIM), F32))
    local_row = lax.broadcasted_iota(jnp.int32, (1, V_DIM), 1) & (HEAD_DIM - 1)
    slope_parts = jnp.zeros((1, V_DIM), F32)
    rest = slope
    for part in range(N_SLOPE_PARTS):
        piece = rest.astype(BF16).astype(F32)
        pairs_with = (local_row == part) | (local_row == part + N_SLOPE_PARTS)
        slope_parts = jnp.where(pairs_with, piece, slope_parts)
        rest = rest - piece
    slope_parts = slope_parts.astype(BF16)

    lane = lax.broadcasted_iota(jnp.int32, (tq, V_DIM), 1)
    lane_k = lax.broadcasted_iota(jnp.int32, (seq, V_DIM), 1)
    k_all = k_ref[...]
    ka0_ref[...] = jnp.where(lane_k < HEAD_DIM, k_all, kpos_ref[...])
    ka1_ref[...] = jnp.where(lane_k >= HEAD_DIM, k_all, kpos_ref[...])
    slope_rows = jnp.broadcast_to(slope_parts, (tq, V_DIM))
    vsum_ref[:V_DIM, :] = vt_ref[...]
    ones_row = lax.broadcasted_iota(jnp.int32, (V_AUG - V_DIM, seq), 0) == 0
    vsum_ref[V_DIM:, :] = jnp.where(ones_row, 1.0, 0.0).astype(BF16)
    causal = (lax.broadcasted_iota(jnp.int32, (tk, tq), 0)
              <= lax.broadcasted_iota(jnp.int32, (tk, tq), 1))
    contract_last = (((1,), (1,)), ((), ()))
    subg = subg_ref[...]
    pack_rows = 2 * V7X_SUBLANES

    def row_group_reduce(op, acc, s):
        for g in range(s.shape[0] // V7X_SUBLANES):
            chunk = s[g * V7X_SUBLANES:(g + 1) * V7X_SUBLANES, :]
            acc = chunk if acc is None else op(acc, chunk)
        return acc

    n_q = seq // tq
    tiles = [dict() for _ in range(n_q)]

    def pass1_tile(i, j):
        st = tiles[i]
        if j == 0:
            q = q_ref[i * tq:(i + 1) * tq, :]
            st["q0"] = jnp.where(lane < HEAD_DIM, q, slope_rows)
            st["q1"] = jnp.where(lane >= HEAD_DIM, q, slope_rows)
            st["m0"] = st["m1"] = None
        kv = slice(j * tk, (j + 1) * tk)
        s0 = lax.dot_general(ka0_ref[kv, :], st["q0"], contract_last, preferred_element_type=F32)
        s1 = lax.dot_general(ka1_ref[kv, :], st["q1"], contract_last, preferred_element_type=F32)
        if j == i:
            s0 = jnp.where(causal, s0, -jnp.inf)
            s1 = jnp.where(causal, s1, -jnp.inf)
        s0_ref[i % N_SCORE_BUFS, j] = s0
        s1_ref[i % N_SCORE_BUFS, j] = s1
        st["m0"] = row_group_reduce(jnp.maximum, st["m0"], s0)
        st["m1"] = row_group_reduce(jnp.maximum, st["m1"], s1)

    def pass2_tile(i, j):
        st = tiles[i]
        buf = i % N_SCORE_BUFS
        if j == 0:
            for m in ("m0", "m1"):
                st[m] = jnp.broadcast_to(jnp.max(st[m], axis=0, keepdims=True), (pack_rows, tq))
        for g in range(tk // pack_rows):
            rows = slice(g * pack_rows, (g + 1) * pack_rows)
            p0_ref[buf, j, rows, :] = jnp.exp2(s0_ref[buf, j, rows, :] - st["m0"]).astype(BF16)
            p1_ref[buf, j, rows, :] = jnp.exp2(s1_ref[buf, j, rows, :] - st["m1"]).astype(BF16)

    def pass3_tile(i, j):
        st = tiles[i]
        buf = i % N_SCORE_BUFS
        kv = slice(j * tk, (j + 1) * tk)
        pv0 = _dot(vsum_ref[:, kv], p0_ref[buf, j])
        pv1 = _dot(vsum_ref[:, kv], p1_ref[buf, j])
        st["o0"] = pv0 if j == 0 else st["o0"] + pv0
        st["o1"] = pv1 if j == 0 else st["o1"] + pv1
        if j == i:
            l0 = st["o0"][V_DIM:V_DIM + 1, :]
            l1 = st["o1"][V_DIM:V_DIM + 1, :]
            o = st["o0"][:V_DIM, :] * (1.0 / l0) - st["o1"][:V_DIM, :] * (lam / l1)
            o = o * lax.rsqrt(jnp.mean(o * o, axis=0, keepdims=True) + LN_EPS)
            o = o * subg * (1.0 - lam_init)
            o_ref[i * tq:(i + 1) * tq, :] = o.T.astype(BF16)
            st.clear()

    for stage in range(n_q + 2):
        for j in range(n_q):
            for phase, fn in enumerate((pass1_tile, pass2_tile, pass3_tile)):
                i = stage - phase
                if 0 <= i < n_q and j <= i:
                    fn(i, j)


def _diff_attention(q, k, v_t, lq1, lk1, lq2, lk2, subln_g, lam_init):
    B, S, _ = q.shape
    tq, tk = Q_TILE, KV_TILE
    head_spec = pl.BlockSpec((None, S, V_DIM), lambda b, h: (b, 0, h))
    small = lambda n: pl.BlockSpec((1, n), lambda b, h: (0, 0))
    tile_stack = (N_SCORE_BUFS, S // tk, tk, tq)
    return pl.pallas_call(
        functools.partial(_diff_attention_kernel, lam_init),
        grid=(B, N_HEADS),
        in_specs=[head_spec, head_spec, pl.BlockSpec((None, V_DIM, S), lambda b, h: (b, h, 0)),
                  small(HEAD_DIM), small(HEAD_DIM), small(HEAD_DIM), small(HEAD_DIM),
                  pl.BlockSpec((V_DIM, 1), lambda b, h: (0, 0))],
        out_specs=head_spec,
        out_shape=jax.ShapeDtypeStruct((B, S, ATTN_WIDTH), BF16),
        scratch_shapes=[
            pltpu.VMEM((S, V_DIM), BF16),
            pltpu.VMEM((S, V_DIM), BF16),
            pltpu.VMEM((S, V_DIM), BF16),
            pltpu.VMEM((V_AUG, S), BF16),
            pltpu.VMEM(tile_stack, F32),
            pltpu.VMEM(tile_stack, F32),
            pltpu.VMEM(tile_stack, BF16),
            pltpu.VMEM(tile_stack, BF16),
        ],
        compiler_params=pltpu.CompilerParams(
            dimension_semantics=("arbitrary", "arbitrary"),
            vmem_limit_bytes=32 * 1024 * 1024,
        ),
        name="diff_attention",
    )(q, k, v_t, lq1, lk1, lq2, lk2, subln_g)


def _layer_norm(y, gain, bias):
    mu = jnp.mean(y, axis=-1, keepdims=True)
    yc = y - mu
    var = jnp.mean(yc * yc, axis=-1, keepdims=True)
    return yc * lax.rsqrt(var + LN_EPS) * gain + bias


def _merge_ffn_kernel(x_ref, attn_ref, rec_ref, gl_ref, bgate_ref, wba_ref, wbr_ref, wout_ref,
                      ln1g_ref, ln1b_ref, wgate_ref, wup_ref, wdown_ref, ln2g_ref, ln2b_ref,
                      o_ref, x1b_ref):
    def sub_tile(rows):
        a = _dot(attn_ref[rows, :], wba_ref[...])
        r = _dot(rec_ref[rows, :], wbr_ref[...])
        yield
        g = jax.nn.sigmoid(gl_ref[rows, :] + bgate_ref[...])
        merged = (g[:, :D_MODEL] * a + g[:, D_MODEL:] * r).astype(BF16)
        yield
        mix = _dot(merged, wout_ref[...])
        yield
        x1 = _layer_norm(ALPHA * x_ref[rows, :] + mix, ln1g_ref[...], ln1b_ref[...])
        x1b_ref[rows, :] = x1.astype(BF16)
        yield
        ffn = None
        for lo, hi in FF_CHUNKS:
            gate = _dot(x1b_ref[rows, :], wgate_ref[:, lo:hi])
            up = _dot(x1b_ref[rows, :], wup_ref[:, lo:hi])
            yield
            hidden = (jax.nn.silu(gate) * up).astype(BF16)
            yield
            down = _dot(hidden, wdown_ref[lo:hi, :])
            ffn = down if ffn is None else ffn + down
        yield
        o_ref[rows, :] = _layer_norm(ALPHA * x1 + ffn, ln2g_ref[...], ln2b_ref[...])
        yield

    tm = x_ref.shape[0]
    slabs = [sub_tile(slice(r0, r0 + SUB_ROWS)) for r0 in range(0, tm, SUB_ROWS)]
    live = list(range(len(slabs)))
    step = 0
    while live:
        for idx in list(live):
            if step >= idx and next(slabs[idx], "done") == "done":
                live.remove(idx)
        step += 1


def _merge_ffn(x2d, attn2d, rec2d, gl2d, b_gate, wba, wbr, wout, ln1_g, ln1_b, wgate, wup, wdown, ln2_g, ln2_b):
    N, D = x2d.shape
    tm = ROW_TILE
    row = lambda width: pl.BlockSpec((tm, width), lambda i: (i, 0))
    return pl.pallas_call(
        _merge_ffn_kernel,
        grid=(N // tm,),
        in_specs=[
            row(D), row(ATTN_WIDTH), row(REC_WIDTH), row(2 * D_MODEL), _resident((1, 2 * D_MODEL)),
            _resident((ATTN_WIDTH, D)), _resident((REC_WIDTH, D)), _resident((D, D)),
            _resident((1, D)), _resident((1, D)),
            _resident((D, D_FF)), _resident((D, D_FF)), _resident((D_FF, D)),
            _resident((1, D)), _resident((1, D)),
        ],
        out_specs=row(D),
        out_shape=jax.ShapeDtypeStruct((N, D), F32),
        scratch_shapes=[pltpu.VMEM((tm, D), BF16)],
        compiler_params=pltpu.CompilerParams(
            dimension_semantics=("arbitrary",),
            vmem_limit_bytes=60 * 1024 * 1024,
        ),
        name="merge_ffn",
    )(x2d, attn2d, rec2d, gl2d, b_gate, wba, wbr, wout, ln1_g, ln1_b, wgate, wup, wdown, ln2_g, ln2_b)


def _pack_block_diag(w):
    w = w.reshape(GATE_TILES, BLOCKS_PER_TILE, REC_BLOCK_DIM, REC_BLOCK_DIM)
    eye = jnp.eye(BLOCKS_PER_TILE, dtype=w.dtype)
    tiles = jnp.einsum("tbij,bc->tbicj", w, eye)
    return tiles.reshape(GATE_TILES, GATE_TILE, GATE_TILE)


def kernel(x, w_in, b_gate, lambda_q1, lambda_k1, lambda_q2, lambda_k2, subln_g, conv_w, conv_b,
           w_a, b_a, w_i, b_i, lru_lambda, w_br_attn, w_br_rec, w_out, ln1_g, ln1_b,
           w_gate, w_up, w_down, ln2_g, ln2_b):
    B, S, D = x.shape
    assert DEPTH == 1 and w_in.shape[0] == DEPTH
    l = 0
    lam_init = 0.8 - 0.6 * math.exp(-0.3 * l)
    row = lambda p: p[l].reshape(1, -1)

    q, k, v_t, rec, gate_logits = _in_proj_recurrent(
        x, w_in[l].astype(BF16), conv_w[l], row(conv_b),
        _pack_block_diag(w_a[l]).astype(BF16), row(b_a),
        _pack_block_diag(w_i[l]).astype(BF16), row(b_i), row(lru_lambda))

    attn = _diff_attention(q, k, v_t, row(lambda_q1), row(lambda_k1), row(lambda_q2), row(lambda_k2),
                           subln_g[l].reshape(V_DIM, 1), lam_init)

    N = B * S
    out = _merge_ffn(
        x.reshape(N, D), attn.reshape(N, ATTN_WIDTH), rec.reshape(N, REC_WIDTH),
        gate_logits.reshape(N, 2 * D_MODEL), row(b_gate),
        w_br_attn[l].astype(BF16), w_br_rec[l].astype(BF16), w_out[l].astype(BF16),
        row(ln1_g), row(ln1_b),
        w_gate[l].astype(BF16), w_up[l].astype(BF16), w_down[l].astype(BF16),
        row(ln2_g), row(ln2_b))
    return out.reshape(B, S, D)
```

```python
import functools
import math

import jax
import jax.numpy as jnp
from jax import lax
from jax.experimental import pallas as pl
from jax.experimental.pallas import tpu as pltpu

D_MODEL = 1024
N_HEADS = 8
HEAD_DIM = 64
V_DIM = 2 * HEAD_DIM
QK_WIDTH = N_HEADS * 2 * HEAD_DIM
ATTN_WIDTH = N_HEADS * V_DIM
REC_WIDTH = 1024
REC_BLOCKS = 16
REC_BLOCK_DIM = REC_WIDTH // REC_BLOCKS
CONV_WIDTH = 4
RG_LRU_C = 8.0
LOG2E = math.log2(math.e)
D_FF = 2816
LN_EPS = 1e-5
DEPTH = 1
ALPHA = (2.0 * DEPTH) ** 0.25
Q_OFF = 0
K_OFF = Q_OFF + QK_WIDTH
V_OFF = K_OFF + QK_WIDTH
XR_OFF = V_OFF + ATTN_WIDTH
YR_OFF = XR_OFF + REC_WIDTH
G_OFF = YR_OFF + REC_WIDTH
IN_WIDTH = G_OFF + 2 * D_MODEL

V7X_LANES = 128
V7X_SUBLANES = 8
V7X_MXU_DIM = 256
V7X_VMEM_BYTES = 64 * 1024 * 1024

GATE_TILE = V7X_MXU_DIM
GATE_TILES = REC_WIDTH // GATE_TILE
BLOCKS_PER_TILE = GATE_TILE // REC_BLOCK_DIM

SEQ_TILE = 512
Q_TILE = 256
KV_TILE = 256
N_SCORE_BUFS = 3
V_AUG = V_DIM + 2 * V7X_SUBLANES
N_SLOPE_PARTS = 3
KPOS_SPLIT = 16
ROW_TILE = 512
SUB_ROWS = 256
FF_CHUNKS = ((0, 1536), (1536, D_FF))

MIB = 1024 * 1024
VMEM_LIMIT = {"in_proj_recurrent": 48 * MIB, "diff_attention": 32 * MIB, "merge_ffn": 60 * MIB}
assert max(VMEM_LIMIT.values()) < V7X_VMEM_BYTES

BF16 = jnp.bfloat16
F32 = jnp.float32


def _dot(a, b):
    return jnp.dot(a, b, preferred_element_type=F32)


def _resident(shape):
    nd = len(shape)
    return pl.BlockSpec(shape, lambda *_: (0,) * nd, pipeline_mode=pl.Buffered(1))


def _gelu_tanh(x):
    c = -2.0 * math.sqrt(2.0 / math.pi) * LOG2E
    neg_2y_log2 = x * (c + (0.044715 * c) * (x * x))
    return x / (1.0 + jnp.exp2(neg_2y_log2))


def _in_proj_recurrent_kernel(x_ref, w_ref, convw_ref, convb_ref, wa_ref, ba_ref,
                              wi_ref, bi_ref, lam_ref,
                              q_ref, k_ref, vt_ref, rec_ref, g_ref,
                              xb_ref, conv_ref, xr_carry, h_carry):
    ts = x_ref.shape[0]

    @pl.when(pl.program_id(1) == 0)
    def _():
        xr_carry[...] = jnp.zeros_like(xr_carry)
        h_carry[...] = jnp.zeros_like(h_carry)

    xb_ref[...] = x_ref[...].astype(BF16)
    row8 = lax.broadcasted_iota(jnp.int32, (V7X_SUBLANES, GATE_TILE), 0)

    def project(off, width=GATE_TILE):
        return _dot(xb_ref[...], w_ref[:, off:off + width])

    def independent_projections():
        for c in range(0, QK_WIDTH, GATE_TILE):
            q_ref[:, c:c + GATE_TILE] = (project(Q_OFF + c) * (LOG2E * HEAD_DIM ** -0.5)).astype(BF16)
            yield
        for c in range(0, QK_WIDTH, GATE_TILE):
            k_ref[:, c:c + GATE_TILE] = project(K_OFF + c).astype(BF16)
            yield
        for c in range(0, ATTN_WIDTH, GATE_TILE):
            vt_ref[c:c + GATE_TILE, :] = project(V_OFF + c).T.astype(BF16)
            yield
        for c in range(0, 2 * D_MODEL, GATE_TILE):
            g_ref[:, c:c + GATE_TILE] = project(G_OFF + c)
            yield

    def recurrent_chain(t, xr):
        cols = slice(t * GATE_TILE, (t + 1) * GATE_TILE)
        slabs = []
        for lt in range(GATE_TILE // V7X_LANES):
            slab = t * (GATE_TILE // V7X_LANES) + lt
            lanes = slice(slab * V7X_LANES, (slab + 1) * V7X_LANES)
            piece = xr[:, lt * V7X_LANES:(lt + 1) * V7X_LANES]
            conv_ref[slab, :V7X_SUBLANES, :] = xr_carry[:, lanes]
            conv_ref[slab, V7X_SUBLANES:, :] = piece
            xr_carry[:, lanes] = piece[ts - V7X_SUBLANES:, :]
            acc = piece * convw_ref[CONV_WIDTH - 1:CONV_WIDTH, lanes] + convb_ref[:, lanes]
            for shift in range(1, CONV_WIDTH):
                shifted = conv_ref[slab, pl.ds(V7X_SUBLANES - shift, ts), :]
                acc = acc + shifted * convw_ref[CONV_WIDTH - 1 - shift:CONV_WIDTH - shift, lanes]
            slabs.append(acc)
        xc = jnp.concatenate(slabs, axis=1)
        yield

        xcb = xc.astype(BF16)
        r = jax.nn.sigmoid(_dot(xcb, wa_ref[t]) + ba_ref[:, cols])
        gate_i = jax.nn.sigmoid(_dot(xcb, wi_ref[t]) + bi_ref[:, cols])
        neg_lam = -lam_ref[:, cols]
        softplus = jnp.maximum(neg_lam, 0.0) + jnp.log1p(jnp.exp(-jnp.abs(neg_lam)))
        decay = (RG_LRU_C * softplus) * r
        a = jnp.exp2(decay * (-LOG2E))
        var = jnp.tanh(decay) * (a * a + 1.0)
        std = jnp.where(var > 0.0, var * lax.rsqrt(var), 0.0)
        u = std * (gate_i * xc)
        yield

        h_prev = h_carry[:, cols]
        hs = []
        n_groups = ts // V7X_SUBLANES
        for grp in range(n_groups):
            rows = slice(grp * V7X_SUBLANES, (grp + 1) * V7X_SUBLANES)
            ag, hg = a[rows, :], u[rows, :]
            for d in (1, 2, 4):
                keep = row8 >= d
                a_s = jnp.where(keep, pltpu.roll(ag, d, 0), 1.0)
                h_s = jnp.where(keep, pltpu.roll(hg, d, 0), 0.0)
                hg = hg + ag * h_s
                ag = ag * a_s
            hg = hg + ag * h_prev
            h_prev = hg[V7X_SUBLANES - 1:, :]
            hs.append(hg)
            if (grp + 1) % (n_groups // 2) == 0:
                yield
        h_carry[:, cols] = h_prev
        h = jnp.concatenate(hs, axis=0)

        yr = project(YR_OFF + t * GATE_TILE)
        rec_ref[:, cols] = (h * _gelu_tanh(yr)).astype(BF16)
        yield

    filler = independent_projections()
    xrs = [project(XR_OFF + t * GATE_TILE) for t in range(GATE_TILES)]
    for t in range(GATE_TILES):
        for _ in recurrent_chain(t, xrs[t]):
            next(filler, None)
    for _ in filler:
        pass


def _in_proj_recurrent(x, w_in, conv_w, conv_b, wa, b_a, wi, b_i, lru_lambda):
    B, S, D = x.shape
    ts = SEQ_TILE
    row = lambda width: pl.BlockSpec((None, ts, width), lambda b, i: (b, i, 0))
    out_shape = (
        jax.ShapeDtypeStruct((B, S, QK_WIDTH), BF16),
        jax.ShapeDtypeStruct((B, S, QK_WIDTH), BF16),
        jax.ShapeDtypeStruct((B, ATTN_WIDTH, S), BF16),
        jax.ShapeDtypeStruct((B, S, REC_WIDTH), BF16),
        jax.ShapeDtypeStruct((B, S, 2 * D_MODEL), F32),
    )
    return pl.pallas_call(
        _in_proj_recurrent_kernel,
        grid=(B, S // ts),
        in_specs=[
            row(D),
            _resident((D, IN_WIDTH)),
            _resident((CONV_WIDTH, REC_WIDTH)),
            _resident((1, REC_WIDTH)),
            _resident((GATE_TILES, GATE_TILE, GATE_TILE)),
            _resident((1, REC_WIDTH)),
            _resident((GATE_TILES, GATE_TILE, GATE_TILE)),
            _resident((1, REC_WIDTH)),
            _resident((1, REC_WIDTH)),
        ],
        out_specs=(row(QK_WIDTH), row(QK_WIDTH),
                   pl.BlockSpec((None, ATTN_WIDTH, ts), lambda b, i: (b, 0, i)),
                   row(REC_WIDTH), row(2 * D_MODEL)),
        out_shape=out_shape,
        scratch_shapes=[
            pltpu.VMEM((ts, D), BF16),
            pltpu.VMEM((REC_WIDTH // V7X_LANES, V7X_SUBLANES + ts, V7X_LANES), F32),
            pltpu.VMEM((V7X_SUBLANES, REC_WIDTH), F32),
            pltpu.VMEM((1, REC_WIDTH), F32),
        ],
        compiler_params=pltpu.CompilerParams(
            dimension_semantics=("arbitrary", "arbitrary"),
            vmem_limit_bytes=VMEM_LIMIT["in_proj_recurrent"],
        ),
        name="in_proj_recurrent",
    )(x, w_in, conv_w, conv_b, wa, b_a, wi, b_i, lru_lambda)


def _diff_attention_kernel(lam_init, q_ref, k_ref, vt_ref, lq1_ref, lk1_ref, lq2_ref, lk2_ref, subg_ref,
                           o_ref, kpos_ref, ka0_ref, ka1_ref, vsum_ref, s0_ref, s1_ref, p0_ref, p1_ref):
    tq, tk = Q_TILE, KV_TILE
    seq = q_ref.shape[0]
    head = pl.program_id(1)

    @pl.when((pl.program_id(0) == 0) & (head == 0))
    def _():
        kpos = lax.broadcasted_iota(jnp.int32, (seq, V_DIM), 0)
        local = lax.broadcasted_iota(jnp.int32, (seq, V_DIM), 1) & (HEAD_DIM - 1)
        fine_bits = kpos & (KPOS_SPLIT - 1)
        coarse = (kpos - fine_bits).astype(F32)
        fine = fine_bits.astype(F32)
        parts = jnp.where(local < N_SLOPE_PARTS, coarse,
                          jnp.where(local < 2 * N_SLOPE_PARTS, fine, 0.0))
        kpos_ref[...] = parts.astype(BF16)

    lam = (jnp.exp(jnp.sum(lq1_ref[...] * lk1_ref[...], axis=-1, keepdims=True))
           - jnp.exp(jnp.sum(lq2_ref[...] * lk2_ref[...], axis=-1, keepdims=True))
           + lam_init)

    slope = LOG2E * jnp.exp2(-(head + 1).astype(F32) * jnp.ones((1, V_DIM), F32))
    local_row = lax.broadcasted_iota(jnp.int32, (1, V_DIM), 1) & (HEAD_DIM - 1)
    slope_parts = jnp.zeros((1, V_DIM), F32)
    rest = slope
    for part in range(N_SLOPE_PARTS):
        piece = rest.astype(BF16).astype(F32)
        pairs_with = (local_row == part) | (local_row == part + N_SLOPE_PARTS)
        slope_parts = jnp.where(pairs_with, piece, slope_parts)
        rest = rest - piece
    slope_parts = slope_parts.astype(BF16)

    lane = lax.broadcasted_iota(jnp.int32, (tq, V_DIM), 1)
    lane_k = lax.broadcasted_iota(jnp.int32, (seq, V_DIM), 1)
    k_all = k_ref[...]
    ka0_ref[...] = jnp.where(lane_k < HEAD_DIM, k_all, kpos_ref[...])
    ka1_ref[...] = jnp.where(lane_k >= HEAD_DIM, k_all, kpos_ref[...])
    slope_rows = jnp.broadcast_to(slope_parts, (tq, V_DIM))
    vsum_ref[:V_DIM, :] = vt_ref[...]
    ones_row = lax.broadcasted_iota(jnp.int32, (V_AUG - V_DIM, seq), 0) == 0
    vsum_ref[V_DIM:, :] = jnp.where(ones_row, 1.0, 0.0).astype(BF16)
    causal = (lax.broadcasted_iota(jnp.int32, (tk, tq), 0)
              <= lax.broadcasted_iota(jnp.int32, (tk, tq), 1))
    contract_last = (((1,), (1,)), ((), ()))
    subg = subg_ref[...]
    pack_rows = 2 * V7X_SUBLANES

    def row_group_reduce(op, acc, s):
        for g in range(s.shape[0] // V7X_SUBLANES):
            chunk = s[g * V7X_SUBLANES:(g + 1) * V7X_SUBLANES, :]
            acc = chunk if acc is None else op(acc, chunk)
        return acc

    n_q = seq // tq
    tiles = [dict() for _ in range(n_q)]

    def pass1_tile(i, j):
        st = tiles[i]
        if j == 0:
            q = q_ref[i * tq:(i + 1) * tq, :]
            st["q0"] = jnp.where(lane < HEAD_DIM, q, slope_rows)
            st["q1"] = jnp.where(lane >= HEAD_DIM, q, slope_rows)
            st["m0"] = st["m1"] = None
        kv = slice(j * tk, (j + 1) * tk)
        s0 = lax.dot_general(ka0_ref[kv, :], st["q0"], contract_last, preferred_element_type=F32)
        s1 = lax.dot_general(ka1_ref[kv, :], st["q1"], contract_last, preferred_element_type=F32)
        if j == i:
            s0 = jnp.where(causal, s0, -jnp.inf)
            s1 = jnp.where(causal, s1, -jnp.inf)
        s0_ref[i % N_SCORE_BUFS, j] = s0
        s1_ref[i % N_SCORE_BUFS, j] = s1
        st["m0"] = row_group_reduce(jnp.maximum, st["m0"], s0)
        st["m1"] = row_group_reduce(jnp.maximum, st["m1"], s1)

    def pass2_tile(i, j):
        st = tiles[i]
        buf = i % N_SCORE_BUFS
        if j == 0:
            for m in ("m0", "m1"):
                st[m] = jnp.broadcast_to(jnp.max(st[m], axis=0, keepdims=True), (pack_rows, tq))
        for g in range(tk // pack_rows):
            rows = slice(g * pack_rows, (g + 1) * pack_rows)
            p0_ref[buf, j, rows, :] = jnp.exp2(s0_ref[buf, j, rows, :] - st["m0"]).astype(BF16)
            p1_ref[buf, j, rows, :] = jnp.exp2(s1_ref[buf, j, rows, :] - st["m1"]).astype(BF16)

    def pass3_tile(i, j):
        st = tiles[i]
        buf = i % N_SCORE_BUFS
        kv = slice(j * tk, (j + 1) * tk)
        pv0 = _dot(vsum_ref[:, kv], p0_ref[buf, j])
        pv1 = _dot(vsum_ref[:, kv], p1_ref[buf, j])
        st["o0"] = pv0 if j == 0 else st["o0"] + pv0
        st["o1"] = pv1 if j == 0 else st["o1"] + pv1
        if j == i:
            l0 = st["o0"][V_DIM:V_DIM + 1, :]
            l1 = st["o1"][V_DIM:V_DIM + 1, :]
            o = st["o0"][:V_DIM, :] * (1.0 / l0) - st["o1"][:V_DIM, :] * (lam / l1)
            o = o * lax.rsqrt(jnp.mean(o * o, axis=0, keepdims=True) + LN_EPS)
            o = o * subg * (1.0 - lam_init)
            o_ref[i * tq:(i + 1) * tq, :] = o.T.astype(BF16)
            st.clear()

    for stage in range(n_q + 2):
        for j in range(n_q):
            for phase, fn in enumerate((pass1_tile, pass2_tile, pass3_tile)):
                i = stage - phase
                if 0 <= i < n_q and j <= i:
                    fn(i, j)


def _diff_attention(q, k, v_t, lq1, lk1, lq2, lk2, subln_g, lam_init):
    B, S, _ = q.shape
    tq, tk = Q_TILE, KV_TILE
    head_spec = pl.BlockSpec((None, S, V_DIM), lambda b, h: (b, 0, h))
    small = lambda n: pl.BlockSpec((1, n), lambda b, h: (0, 0))
    tile_stack = (N_SCORE_BUFS, S // tk, tk, tq)
    return pl.pallas_call(
        functools.partial(_diff_attention_kernel, lam_init),
        grid=(B, N_HEADS),
        in_specs=[head_spec, head_spec, pl.BlockSpec((None, V_DIM, S), lambda b, h: (b, h, 0)),
                  small(HEAD_DIM), small(HEAD_DIM), small(HEAD_DIM), small(HEAD_DIM),
                  pl.BlockSpec((V_DIM, 1), lambda b, h: (0, 0))],
        out_specs=head_spec,
        out_shape=jax.ShapeDtypeStruct((B, S, ATTN_WIDTH), BF16),
        scratch_shapes=[
            pltpu.VMEM((S, V_DIM), BF16),
            pltpu.VMEM((S, V_DIM), BF16),
            pltpu.VMEM((S, V_DIM), BF16),
            pltpu.VMEM((V_AUG, S), BF16),
            pltpu.VMEM(tile_stack, F32),
            pltpu.VMEM(tile_stack, F32),
            pltpu.VMEM(tile_stack, BF16),
            pltpu.VMEM(tile_stack, BF16),
        ],
        compiler_params=pltpu.CompilerParams(
            dimension_semantics=("arbitrary", "arbitrary"),
            vmem_limit_bytes=VMEM_LIMIT["diff_attention"],
        ),
        name="diff_attention",
    )(q, k, v_t, lq1, lk1, lq2, lk2, subln_g)


def _layer_norm(y, gain, bias):
    mu = jnp.mean(y, axis=-1, keepdims=True)
    yc = y - mu
    var = jnp.mean(yc * yc, axis=-1, keepdims=True)
    return yc * lax.rsqrt(var + LN_EPS) * gain + bias


def _merge_ffn_kernel(x_ref, attn_ref, rec_ref, gl_ref, bgate_ref, wba_ref, wbr_ref, wout_ref,
                      ln1g_ref, ln1b_ref, wgate_ref, wup_ref, wdown_ref, ln2g_ref, ln2b_ref,
                      o_ref, x1b_ref):
    def sub_tile(rows):
        a = _dot(attn_ref[rows, :], wba_ref[...])
        r = _dot(rec_ref[rows, :], wbr_ref[...])
        yield
        g = jax.nn.sigmoid(gl_ref[rows, :] + bgate_ref[...])
        merged = (g[:, :D_MODEL] * a + g[:, D_MODEL:] * r).astype(BF16)
        yield
        mix = _dot(merged, wout_ref[...])
        yield
        x1 = _layer_norm(ALPHA * x_ref[rows, :] + mix, ln1g_ref[...], ln1b_ref[...])
        x1b_ref[rows, :] = x1.astype(BF16)
        yield
        ffn = None
        for lo, hi in FF_CHUNKS:
            gate = _dot(x1b_ref[rows, :], wgate_ref[:, lo:hi])
            up = _dot(x1b_ref[rows, :], wup_ref[:, lo:hi])
            yield
            hidden = (jax.nn.silu(gate) * up).astype(BF16)
            yield
            down = _dot(hidden, wdown_ref[lo:hi, :])
            ffn = down if ffn is None else ffn + down
        yield
        o_ref[rows, :] = _layer_norm(ALPHA * x1 + ffn, ln2g_ref[...], ln2b_ref[...])
        yield

    tm = x_ref.shape[0]
    slabs = [sub_tile(slice(r0, r0 + SUB_ROWS)) for r0 in range(0, tm, SUB_ROWS)]
    live = list(range(len(slabs)))
    step = 0
    while live:
        for idx in list(live):
            if step >= idx and next(slabs[idx], "done") == "done":
                live.remove(idx)
        step += 1


def _merge_ffn(x2d, attn2d, rec2d, gl2d, b_gate, wba, wbr, wout, ln1_g, ln1_b, wgate, wup, wdown, ln2_g, ln2_b):
    N, D = x2d.shape
    tm = ROW_TILE
    row = lambda width: pl.BlockSpec((tm, width), lambda i: (i, 0))
    return pl.pallas_call(
        _merge_ffn_kernel,
        grid=(N // tm,),
        in_specs=[
            row(D), row(ATTN_WIDTH), row(REC_WIDTH), row(2 * D_MODEL), _resident((1, 2 * D_MODEL)),
            _resident((ATTN_WIDTH, D)), _resident((REC_WIDTH, D)), _resident((D, D)),
            _resident((1, D)), _resident((1, D)),
            _resident((D, D_FF)), _resident((D, D_FF)), _resident((D_FF, D)),
            _resident((1, D)), _resident((1, D)),
        ],
        out_specs=row(D),
        out_shape=jax.ShapeDtypeStruct((N, D), F32),
        scratch_shapes=[pltpu.VMEM((tm, D), BF16)],
        compiler_params=pltpu.CompilerParams(
            dimension_semantics=("arbitrary",),
            vmem_limit_bytes=VMEM_LIMIT["merge_ffn"],
        ),
        name="merge_ffn",
    )(x2d, attn2d, rec2d, gl2d, b_gate, wba, wbr, wout, ln1_g, ln1_b, wgate, wup, wdown, ln2_g, ln2_b)


def _pack_block_diag(w):
    w = w.reshape(GATE_TILES, BLOCKS_PER_TILE, REC_BLOCK_DIM, REC_BLOCK_DIM)
    eye = jnp.eye(BLOCKS_PER_TILE, dtype=w.dtype)
    tiles = jnp.einsum("tbij,bc->tbicj", w, eye)
    return tiles.reshape(GATE_TILES, GATE_TILE, GATE_TILE)


def kernel(x, w_in, b_gate, lambda_q1, lambda_k1, lambda_q2, lambda_k2, subln_g, conv_w, conv_b,
           w_a, b_a, w_i, b_i, lru_lambda, w_br_attn, w_br_rec, w_out, ln1_g, ln1_b,
           w_gate, w_up, w_down, ln2_g, ln2_b):
    B, S, D = x.shape
    assert DEPTH == 1 and w_in.shape[0] == DEPTH
    l = 0
    lam_init = 0.8 - 0.6 * math.exp(-0.3 * l)
    row = lambda p: p[l].reshape(1, -1)

    q, k, v_t, rec, gate_logits = _in_proj_recurrent(
        x, w_in[l].astype(BF16), conv_w[l], row(conv_b),
        _pack_block_diag(w_a[l]).astype(BF16), row(b_a),
        _pack_block_diag(w_i[l]).astype(BF16), row(b_i), row(lru_lambda))

    attn = _diff_attention(q, k, v_t, row(lambda_q1), row(lambda_k1), row(lambda_q2), row(lambda_k2),
                           subln_g[l].reshape(V_DIM, 1), lam_init)

    N = B * S
    out = _merge_ffn(
        x.reshape(N, D), attn.reshape(N, ATTN_WIDTH), rec.reshape(N, REC_WIDTH),
        gate_logits.reshape(N, 2 * D_MODEL), row(b_gate),
        w_br_attn[l].astype(BF16), w_br_rec[l].astype(BF16), w_out[l].astype(BF16),
        row(ln1_g), row(ln1_b),
        w_gate[l].astype(BF16), w_up[l].astype(BF16), w_down[l].astype(BF16),
        row(ln2_g), row(ln2_b))
    return out.reshape(B, S, D)
```

```python
import functools
import math

import jax
import jax.numpy as jnp
from jax import lax
from jax.experimental import pallas as pl
from jax.experimental.pallas import tpu as pltpu

D_MODEL = 1024
N_HEADS = 8
HEAD_DIM = 64
V_DIM = 2 * HEAD_DIM
QK_WIDTH = N_HEADS * 2 * HEAD_DIM
ATTN_WIDTH = N_HEADS * V_DIM
REC_WIDTH = 1024
REC_BLOCKS = 16
REC_BLOCK_DIM = REC_WIDTH // REC_BLOCKS
CONV_WIDTH = 4
RG_LRU_C = 8.0
LOG2E = math.log2(math.e)
D_FF = 2816
LN_EPS = 1e-5
DEPTH = 1
ALPHA = (2.0 * DEPTH) ** 0.25
Q_OFF = 0
K_OFF = Q_OFF + QK_WIDTH
V_OFF = K_OFF + QK_WIDTH
XR_OFF = V_OFF + ATTN_WIDTH
YR_OFF = XR_OFF + REC_WIDTH
G_OFF = YR_OFF + REC_WIDTH
IN_WIDTH = G_OFF + 2 * D_MODEL

V7X_LANES = 128
V7X_SUBLANES = 8
V7X_MXU_DIM = 256
V7X_VMEM_BYTES = 64 * 1024 * 1024

GATE_TILE = V7X_MXU_DIM
GATE_TILES = REC_WIDTH // GATE_TILE
BLOCKS_PER_TILE = GATE_TILE // REC_BLOCK_DIM

SEQ_TILE = 512
Q_TILE = 256
KV_TILE = 256
N_SCORE_BUFS = 3
V_AUG = V_DIM + 2 * V7X_SUBLANES
N_SLOPE_PARTS = 3
KPOS_SPLIT = 16
ROW_TILE = 512
SUB_ROWS = 256
FF_CHUNKS = ((0, 1536), (1536, D_FF))

MIB = 1024 * 1024
VMEM_LIMIT = {"in_proj_recurrent": 48 * MIB, "diff_attention": 32 * MIB, "merge_ffn": 60 * MIB}
assert max(VMEM_LIMIT.values()) < V7X_VMEM_BYTES

BF16 = jnp.bfloat16
F32 = jnp.float32


def _dot(a, b):
    return jnp.dot(a, b, preferred_element_type=F32)


def _resident(shape):
    nd = len(shape)
    return pl.BlockSpec(shape, lambda *_: (0,) * nd, pipeline_mode=pl.Buffered(1))


def _gelu_tanh(x):
    c = -2.0 * math.sqrt(2.0 / math.pi) * LOG2E
    neg_2y_log2 = x * (c + (0.044715 * c) * (x * x))
    return x / (1.0 + jnp.exp2(neg_2y_log2))


def _in_proj_recurrent_kernel(x_ref, w_ref, convw_ref, convb_ref, wa_ref, ba_ref,
                              wi_ref, bi_ref, lam_ref,
                              q_ref, k_ref, vt_ref, rec_ref, g_ref,
                              xb_ref, conv_ref, scan_a_ref, scan_h_ref, xr_carry, h_carry):
    ts = x_ref.shape[0]

    @pl.when(pl.program_id(1) == 0)
    def _():
        xr_carry[...] = jnp.zeros_like(xr_carry)
        h_carry[...] = jnp.zeros_like(h_carry)

    @pl.when((pl.program_id(0) == 0) & (pl.program_id(1) == 0))
    def _():
        scan_a_ref[...] = jnp.ones_like(scan_a_ref)
        scan_h_ref[...] = jnp.zeros_like(scan_h_ref)

    xb_ref[...] = x_ref[...].astype(BF16)

    def project(off, width=GATE_TILE):
        return _dot(xb_ref[...], w_ref[:, off:off + width])

    def independent_projections():
        for c in range(0, QK_WIDTH, GATE_TILE):
            q_ref[:, c:c + GATE_TILE] = (project(Q_OFF + c) * (LOG2E * HEAD_DIM ** -0.5)).astype(BF16)
            yield
        for c in range(0, QK_WIDTH, GATE_TILE):
            k_ref[:, c:c + GATE_TILE] = project(K_OFF + c).astype(BF16)
            yield
        for c in range(0, ATTN_WIDTH, GATE_TILE):
            vt_ref[c:c + GATE_TILE, :] = project(V_OFF + c).T.astype(BF16)
            yield
        for c in range(0, 2 * D_MODEL, GATE_TILE):
            g_ref[:, c:c + GATE_TILE] = project(G_OFF + c)
            yield

    def recurrent_chain(t, xr):
        cols = slice(t * GATE_TILE, (t + 1) * GATE_TILE)
        slabs = []
        for lt in range(GATE_TILE // V7X_LANES):
            slab = t * (GATE_TILE // V7X_LANES) + lt
            lanes = slice(slab * V7X_LANES, (slab + 1) * V7X_LANES)
            piece = xr[:, lt * V7X_LANES:(lt + 1) * V7X_LANES]
            conv_ref[slab, :V7X_SUBLANES, :] = xr_carry[:, lanes]
            conv_ref[slab, V7X_SUBLANES:, :] = piece
            xr_carry[:, lanes] = piece[ts - V7X_SUBLANES:, :]
            acc = piece * convw_ref[CONV_WIDTH - 1:CONV_WIDTH, lanes] + convb_ref[:, lanes]
            for shift in range(1, CONV_WIDTH):
                shifted = conv_ref[slab, pl.ds(V7X_SUBLANES - shift, ts), :]
                acc = acc + shifted * convw_ref[CONV_WIDTH - 1 - shift:CONV_WIDTH - shift, lanes]
            slabs.append(acc)
        xc = jnp.concatenate(slabs, axis=1)
        yield

        xcb = xc.astype(BF16)
        r = jax.nn.sigmoid(_dot(xcb, wa_ref[t]) + ba_ref[:, cols])
        gate_i = jax.nn.sigmoid(_dot(xcb, wi_ref[t]) + bi_ref[:, cols])
        neg_lam = -lam_ref[:, cols]
        softplus = jnp.maximum(neg_lam, 0.0) + jnp.log1p(jnp.exp(-jnp.abs(neg_lam)))
        decay = (RG_LRU_C * softplus) * r
        a = jnp.exp2(decay * (-LOG2E))
        var = jnp.tanh(decay) * (a * a + 1.0)
        std = jnp.where(var > 0.0, var * lax.rsqrt(var), 0.0)
        u = std * (gate_i * xc)
        yield

        n_groups = ts // V7X_SUBLANES
        pitch = 2 * V7X_SUBLANES
        slab_lanes = [slice(lt * V7X_LANES, (lt + 1) * V7X_LANES) for lt in range(GATE_TILE // V7X_LANES)]
        ag, hg = {}, {}
        for lt, lanes in enumerate(slab_lanes):
            for grp in range(n_groups):
                rows = slice(grp * V7X_SUBLANES, (grp + 1) * V7X_SUBLANES)
                ag[lt, grp], hg[lt, grp] = a[rows, lanes], u[rows, lanes]
        for d in (1, 2, 4):
            for (lt, grp) in ag:
                data = slice(grp * pitch + V7X_SUBLANES, (grp + 1) * pitch)
                scan_a_ref[lt, data, :] = ag[lt, grp]
                scan_h_ref[lt, data, :] = hg[lt, grp]
            for (lt, grp) in ag:
                above = pl.ds(grp * pitch + V7X_SUBLANES - d, V7X_SUBLANES)
                hg[lt, grp] = hg[lt, grp] + ag[lt, grp] * scan_h_ref[lt, above, :]
                ag[lt, grp] = ag[lt, grp] * scan_a_ref[lt, above, :]
            if d < 4:
                yield
        slabs = []
        for lt, lanes in enumerate(slab_lanes):
            carry_lanes = slice(t * GATE_TILE + lt * V7X_LANES, t * GATE_TILE + (lt + 1) * V7X_LANES)
            h_prev = h_carry[:, carry_lanes]
            hs = []
            for grp in range(n_groups):
                h_grp = hg[lt, grp] + ag[lt, grp] * h_prev
                h_prev = h_grp[V7X_SUBLANES - 1:, :]
                hs.append(h_grp)
            h_carry[:, carry_lanes] = h_prev
            slabs.append(jnp.concatenate(hs, axis=0))
        h = jnp.concatenate(slabs, axis=1)

        yr = project(YR_OFF + t * GATE_TILE)
        rec_ref[:, cols] = (h * _gelu_tanh(yr)).astype(BF16)
        yield

    filler = independent_projections()
    xrs = [project(XR_OFF + t * GATE_TILE) for t in range(GATE_TILES)]
    for t in range(GATE_TILES):
        for _ in recurrent_chain(t, xrs[t]):
            next(filler, None)
    for _ in filler:
        pass


def _in_proj_recurrent(x, w_in, conv_w, conv_b, wa, b_a, wi, b_i, lru_lambda):
    B, S, D = x.shape
    ts = SEQ_TILE
    row = lambda width: pl.BlockSpec((None, ts, width), lambda b, i: (b, i, 0))
    out_shape = (
        jax.ShapeDtypeStruct((B, S, QK_WIDTH), BF16),
        jax.ShapeDtypeStruct((B, S, QK_WIDTH), BF16),
        jax.ShapeDtypeStruct((B, ATTN_WIDTH, S), BF16),
        jax.ShapeDtypeStruct((B, S, REC_WIDTH), BF16),
        jax.ShapeDtypeStruct((B, S, 2 * D_MODEL), F32),
    )
    return pl.pallas_call(
        _in_proj_recurrent_kernel,
        grid=(B, S // ts),
        in_specs=[
            row(D),
            _resident((D, IN_WIDTH)),
            _resident((CONV_WIDTH, REC_WIDTH)),
            _resident((1, REC_WIDTH)),
            _resident((GATE_TILES, GATE_TILE, GATE_TILE)),
            _resident((1, REC_WIDTH)),
            _resident((GATE_TILES, GATE_TILE, GATE_TILE)),
            _resident((1, REC_WIDTH)),
            _resident((1, REC_WIDTH)),
        ],
        out_specs=(row(QK_WIDTH), row(QK_WIDTH),
                   pl.BlockSpec((None, ATTN_WIDTH, ts), lambda b, i: (b, 0, i)),
                   row(REC_WIDTH), row(2 * D_MODEL)),
        out_shape=out_shape,
        scratch_shapes=[
            pltpu.VMEM((ts, D), BF16),
            pltpu.VMEM((REC_WIDTH // V7X_LANES, V7X_SUBLANES + ts, V7X_LANES), F32),
            pltpu.VMEM((GATE_TILE // V7X_LANES, 2 * ts, V7X_LANES), F32),
            pltpu.VMEM((GATE_TILE // V7X_LANES, 2 * ts, V7X_LANES), F32),
            pltpu.VMEM((V7X_SUBLANES, REC_WIDTH), F32),
            pltpu.VMEM((1, REC_WIDTH), F32),
        ],
        compiler_params=pltpu.CompilerParams(
            dimension_semantics=("arbitrary", "arbitrary"),
            vmem_limit_bytes=VMEM_LIMIT["in_proj_recurrent"],
        ),
        name="in_proj_recurrent",
    )(x, w_in, conv_w, conv_b, wa, b_a, wi, b_i, lru_lambda)


def _diff_attention_kernel(lam_init, q_ref, k_ref, vt_ref, lq1_ref, lk1_ref, lq2_ref, lk2_ref, subg_ref,
                           o_ref, kpos_ref, ka0_ref, ka1_ref, vsum_ref, s0_ref, s1_ref, p0_ref, p1_ref):
    tq, tk = Q_TILE, KV_TILE
    seq = q_ref.shape[0]
    head = pl.program_id(1)

    @pl.when((pl.program_id(0) == 0) & (head == 0))
    def _():
        kpos = lax.broadcasted_iota(jnp.int32, (seq, V_DIM), 0)
        local = lax.broadcasted_iota(jnp.int32, (seq, V_DIM), 1) & (HEAD_DIM - 1)
        fine_bits = kpos & (KPOS_SPLIT - 1)
        coarse = (kpos - fine_bits).astype(F32)
        fine = fine_bits.astype(F32)
        parts = jnp.where(local < N_SLOPE_PARTS, coarse,
                          jnp.where(local < 2 * N_SLOPE_PARTS, fine, 0.0))
        kpos_ref[...] = parts.astype(BF16)

    lam = (jnp.exp(jnp.sum(lq1_ref[...] * lk1_ref[...], axis=-1, keepdims=True))
           - jnp.exp(jnp.sum(lq2_ref[...] * lk2_ref[...], axis=-1, keepdims=True))
           + lam_init)

    slope = LOG2E * jnp.exp2(-(head + 1).astype(F32) * jnp.ones((1, V_DIM), F32))
    local_row = lax.broadcasted_iota(jnp.int32, (1, V_DIM), 1) & (HEAD_DIM - 1)
    slope_parts = jnp.zeros((1, V_DIM), F32)
    rest = slope
    for part in range(N_SLOPE_PARTS):
        piece = rest.astype(BF16).astype(F32)
        pairs_with = (local_row == part) | (local_row == part + N_SLOPE_PARTS)
        slope_parts = jnp.where(pairs_with, piece, slope_parts)
        rest = rest - piece
    slope_parts = slope_parts.astype(BF16)

    lane = lax.broadcasted_iota(jnp.int32, (tq, V_DIM), 1)
    lane_k = lax.broadcasted_iota(jnp.int32, (seq, V_DIM), 1)
    k_all = k_ref[...]
    ka0_ref[...] = jnp.where(lane_k < HEAD_DIM, k_all, kpos_ref[...])
    ka1_ref[...] = jnp.where(lane_k >= HEAD_DIM, k_all, kpos_ref[...])
    slope_rows = jnp.broadcast_to(slope_parts, (tq, V_DIM))
    vsum_ref[:V_DIM, :] = vt_ref[...]
    ones_row = lax.broadcasted_iota(jnp.int32, (V_AUG - V_DIM, seq), 0) == 0
    vsum_ref[V_DIM:, :] = jnp.where(ones_row, 1.0, 0.0).astype(BF16)
    causal = (lax.broadcasted_iota(jnp.int32, (tk, tq), 0)
              <= lax.broadcasted_iota(jnp.int32, (tk, tq), 1))
    contract_last = (((1,), (1,)), ((), ()))
    subg = subg_ref[...]
    pack_rows = 2 * V7X_SUBLANES

    def row_group_reduce(op, acc, s):
        for g in range(s.shape[0] // V7X_SUBLANES):
            chunk = s[g * V7X_SUBLANES:(g + 1) * V7X_SUBLANES, :]
            acc = chunk if acc is None else op(acc, chunk)
        return acc

    n_q = seq // tq
    tiles = [dict() for _ in range(n_q)]

    def pass1_tile(i, j):
        st = tiles[i]
        if j == 0:
            q = q_ref[i * tq:(i + 1) * tq, :]
            st["q0"] = jnp.where(lane < HEAD_DIM, q, slope_rows)
            st["q1"] = jnp.where(lane >= HEAD_DIM, q, slope_rows)
            st["m0"] = st["m1"] = None
        kv = slice(j * tk, (j + 1) * tk)
        s0 = lax.dot_general(ka0_ref[kv, :], st["q0"], contract_last, preferred_element_type=F32)
        s1 = lax.dot_general(ka1_ref[kv, :], st["q1"], contract_last, preferred_element_type=F32)
        if j == i:
            s0 = jnp.where(causal, s0, -jnp.inf)
            s1 = jnp.where(causal, s1, -jnp.inf)
        s0_ref[i % N_SCORE_BUFS, j] = s0
        s1_ref[i % N_SCORE_BUFS, j] = s1
        st["m0"] = row_group_reduce(jnp.maximum, st["m0"], s0)
        st["m1"] = row_group_reduce(jnp.maximum, st["m1"], s1)

    def pass2_tile(i, j):
        st = tiles[i]
        buf = i % N_SCORE_BUFS
        if j == 0:
            for m in ("m0", "m1"):
                st[m] = jnp.broadcast_to(jnp.max(st[m], axis=0, keepdims=True), (pack_rows, tq))
        for g in range(tk // pack_rows):
            rows = slice(g * pack_rows, (g + 1) * pack_rows)
            p0_ref[buf, j, rows, :] = jnp.exp2(s0_ref[buf, j, rows, :] - st["m0"]).astype(BF16)
            p1_ref[buf, j, rows, :] = jnp.exp2(s1_ref[buf, j, rows, :] - st["m1"]).astype(BF16)

    def pass3_tile(i, j):
        st = tiles[i]
        buf = i % N_SCORE_BUFS
        kv = slice(j * tk, (j + 1) * tk)
        pv0 = _dot(vsum_ref[:, kv], p0_ref[buf, j])
        pv1 = _dot(vsum_ref[:, kv], p1_ref[buf, j])
        st["o0"] = pv0 if j == 0 else st["o0"] + pv0
        st["o1"] = pv1 if j == 0 else st["o1"] + pv1
        if j == i:
            l0 = st["o0"][V_DIM:V_DIM + 1, :]
            l1 = st["o1"][V_DIM:V_DIM + 1, :]
            o = st["o0"][:V_DIM, :] * (1.0 / l0) - st["o1"][:V_DIM, :] * (lam / l1)
            o = o * lax.rsqrt(jnp.mean(o * o, axis=0, keepdims=True) + LN_EPS)
            o = o * subg * (1.0 - lam_init)
            o_ref[i * tq:(i + 1) * tq, :] = o.T.astype(BF16)
            st.clear()

    for stage in range(n_q + 2):
        for j in range(n_q):
            for phase, fn in enumerate((pass1_tile, pass2_tile, pass3_tile)):
                i = stage - phase
                if 0 <= i < n_q and j <= i:
                    fn(i, j)


def _diff_attention(q, k, v_t, lq1, lk1, lq2, lk2, subln_g, lam_init):
    B, S, _ = q.shape
    tq, tk = Q_TILE, KV_TILE
    head_spec = pl.BlockSpec((None, S, V_DIM), lambda b, h: (b, 0, h))
    small = lambda n: pl.BlockSpec((1, n), lambda b, h: (0, 0))
    tile_stack = (N_SCORE_BUFS, S // tk, tk, tq)
    return pl.pallas_call(
        functools.partial(_diff_attention_kernel, lam_init),
        grid=(B, N_HEADS),
        in_specs=[head_spec, head_spec, pl.BlockSpec((None, V_DIM, S), lambda b, h: (b, h, 0)),
                  small(HEAD_DIM), small(HEAD_DIM), small(HEAD_DIM), small(HEAD_DIM),
                  pl.BlockSpec((V_DIM, 1), lambda b, h: (0, 0))],
        out_specs=head_spec,
        out_shape=jax.ShapeDtypeStruct((B, S, ATTN_WIDTH), BF16),
        scratch_shapes=[
            pltpu.VMEM((S, V_DIM), BF16),
            pltpu.VMEM((S, V_DIM), BF16),
            pltpu.VMEM((S, V_DIM), BF16),
            pltpu.VMEM((V_AUG, S), BF16),
            pltpu.VMEM(tile_stack, F32),
            pltpu.VMEM(tile_stack, F32),
            pltpu.VMEM(tile_stack, BF16),
            pltpu.VMEM(tile_stack, BF16),
        ],
        compiler_params=pltpu.CompilerParams(
            dimension_semantics=("arbitrary", "arbitrary"),
            vmem_limit_bytes=VMEM_LIMIT["diff_attention"],
        ),
        name="diff_attention",
    )(q, k, v_t, lq1, lk1, lq2, lk2, subln_g)


def _layer_norm(y, gain, bias):
    mu = jnp.mean(y, axis=-1, keepdims=True)
    yc = y - mu
    var = jnp.mean(yc * yc, axis=-1, keepdims=True)
    return yc * lax.rsqrt(var + LN_EPS) * gain + bias


def _merge_ffn_kernel(x_ref, attn_ref, rec_ref, gl_ref, bgate_ref, wba_ref, wbr_ref, wout_ref,
                      ln1g_ref, ln1b_ref, wgate_ref, wup_ref, wdown_ref, ln2g_ref, ln2b_ref,
                      o_ref, x1b_ref):
    def sub_tile(rows):
        a = _dot(attn_ref[rows, :], wba_ref[...])
        r = _dot(rec_ref[rows, :], wbr_ref[...])
        yield
        g = jax.nn.sigmoid(gl_ref[rows, :] + bgate_ref[...])
        merged = (g[:, :D_MODEL] * a + g[:, D_MODEL:] * r).astype(BF16)
        yield
        mix = _dot(merged, wout_ref[...])
        yield
        x1 = _layer_norm(ALPHA * x_ref[rows, :] + mix, ln1g_ref[...], ln1b_ref[...])
        x1b_ref[rows, :] = x1.astype(BF16)
        yield
        ffn = None
        for lo, hi in FF_CHUNKS:
            gate = _dot(x1b_ref[rows, :], wgate_ref[:, lo:hi])
            up = _dot(x1b_ref[rows, :], wup_ref[:, lo:hi])
            yield
            hidden = (jax.nn.silu(gate) * up).astype(BF16)
            yield
            down = _dot(hidden, wdown_ref[lo:hi, :])
            ffn = down if ffn is None else ffn + down
        yield
        o_ref[rows, :] = _layer_norm(ALPHA * x1 + ffn, ln2g_ref[...], ln2b_ref[...])
        yield

    tm = x_ref.shape[0]
    slabs = [sub_tile(slice(r0, r0 + SUB_ROWS)) for r0 in range(0, tm, SUB_ROWS)]
    live = list(range(len(slabs)))
    step = 0
    while live:
        for idx in list(live):
            if step >= idx and next(slabs[idx], "done") == "done":
                live.remove(idx)
        step += 1


def _merge_ffn(x2d, attn2d, rec2d, gl2d, b_gate, wba, wbr, wout, ln1_g, ln1_b, wgate, wup, wdown, ln2_g, ln2_b):
    N, D = x2d.shape
    tm = ROW_TILE
    row = lambda width: pl.BlockSpec((tm, width), lambda i: (i, 0))
    return pl.pallas_call(
        _merge_ffn_kernel,
        grid=(N // tm,),
        in_specs=[
            row(D), row(ATTN_WIDTH), row(REC_WIDTH), row(2 * D_MODEL), _resident((1, 2 * D_MODEL)),
            _resident((ATTN_WIDTH, D)), _resident((REC_WIDTH, D)), _resident((D, D)),
            _resident((1, D)), _resident((1, D)),
            _resident((D, D_FF)), _resident((D, D_FF)), _resident((D_FF, D)),
            _resident((1, D)), _resident((1, D)),
        ],
        out_specs=row(D),
        out_shape=jax.ShapeDtypeStruct((N, D), F32),
        scratch_shapes=[pltpu.VMEM((tm, D), BF16)],
        compiler_params=pltpu.CompilerParams(
            dimension_semantics=("arbitrary",),
            vmem_limit_bytes=VMEM_LIMIT["merge_ffn"],
        ),
        name="merge_ffn",
    )(x2d, attn2d, rec2d, gl2d, b_gate, wba, wbr, wout, ln1_g, ln1_b, wgate, wup, wdown, ln2_g, ln2_b)


def _pack_block_diag(w):
    w = w.reshape(GATE_TILES, BLOCKS_PER_TILE, REC_BLOCK_DIM, REC_BLOCK_DIM)
    eye = jnp.eye(BLOCKS_PER_TILE, dtype=w.dtype)
    tiles = jnp.einsum("tbij,bc->tbicj", w, eye)
    return tiles.reshape(GATE_TILES, GATE_TILE, GATE_TILE)


def kernel(x, w_in, b_gate, lambda_q1, lambda_k1, lambda_q2, lambda_k2, subln_g, conv_w, conv_b,
           w_a, b_a, w_i, b_i, lru_lambda, w_br_attn, w_br_rec, w_out, ln1_g, ln1_b,
           w_gate, w_up, w_down, ln2_g, ln2_b):
    B, S, D = x.shape
    assert DEPTH == 1 and w_in.shape[0] == DEPTH
    l = 0
    lam_init = 0.8 - 0.6 * math.exp(-0.3 * l)
    row = lambda p: p[l].reshape(1, -1)

    q, k, v_t, rec, gate_logits = _in_proj_recurrent(
        x, w_in[l].astype(BF16), conv_w[l], row(conv_b),
        _pack_block_diag(w_a[l]).astype(BF16), row(b_a),
        _pack_block_diag(w_i[l]).astype(BF16), row(b_i), row(lru_lambda))

    attn = _diff_attention(q, k, v_t, row(lambda_q1), row(lambda_k1), row(lambda_q2), row(lambda_k2),
                           subln_g[l].reshape(V_DIM, 1), lam_init)

    N = B * S
    out = _merge_ffn(
        x.reshape(N, D), attn.reshape(N, ATTN_WIDTH), rec.reshape(N, REC_WIDTH),
        gate_logits.reshape(N, 2 * D_MODEL), row(b_gate),
        w_br_attn[l].astype(BF16), w_br_rec[l].astype(BF16), w_out[l].astype(BF16),
        row(ln1_g), row(ln1_b),
        w_gate[l].astype(BF16), w_up[l].astype(BF16), w_down[l].astype(BF16),
        row(ln2_g), row(ln2_b))
    return out.reshape(B, S, D)
```

```python
import functools
import math

import jax
import jax.numpy as jnp
from jax import lax
from jax.experimental import pallas as pl
from jax.experimental.pallas import tpu as pltpu

D_MODEL = 1024
N_HEADS = 8
HEAD_DIM = 64
V_DIM = 2 * HEAD_DIM
QK_WIDTH = N_HEADS * 2 * HEAD_DIM
ATTN_WIDTH = N_HEADS * V_DIM
REC_WIDTH = 1024
REC_BLOCKS = 16
REC_BLOCK_DIM = REC_WIDTH // REC_BLOCKS
CONV_WIDTH = 4
RG_LRU_C = 8.0
LOG2E = math.log2(math.e)
D_FF = 2816
LN_EPS = 1e-5
DEPTH = 1
ALPHA = (2.0 * DEPTH) ** 0.25
Q_OFF = 0
K_OFF = Q_OFF + QK_WIDTH
V_OFF = K_OFF + QK_WIDTH
XR_OFF = V_OFF + ATTN_WIDTH
YR_OFF = XR_OFF + REC_WIDTH
G_OFF = YR_OFF + REC_WIDTH
IN_WIDTH = G_OFF + 2 * D_MODEL

V7X_LANES = 128
V7X_SUBLANES = 8
V7X_MXU_DIM = 256
V7X_VMEM_BYTES = 64 * 1024 * 1024

GATE_TILE = V7X_MXU_DIM
GATE_TILES = REC_WIDTH // GATE_TILE
BLOCKS_PER_TILE = GATE_TILE // REC_BLOCK_DIM

SEQ_TILE = 1024
Q_TILE = 256
KV_TILE = 256
N_SCORE_BUFS = 3
V_AUG = V_DIM + 2 * V7X_SUBLANES
N_SLOPE_PARTS = 3
KPOS_SPLIT = 16
ROW_TILE = 512
SUB_ROWS = 256
FF_CHUNKS = ((0, 1536), (1536, D_FF))

MIB = 1024 * 1024
VMEM_LIMIT = {"in_proj_recurrent": 48 * MIB, "diff_attention": 32 * MIB, "merge_ffn": 60 * MIB}
assert max(VMEM_LIMIT.values()) < V7X_VMEM_BYTES

BF16 = jnp.bfloat16
F32 = jnp.float32


def _dot(a, b):
    return jnp.dot(a, b, preferred_element_type=F32)


def _resident(shape):
    nd = len(shape)
    return pl.BlockSpec(shape, lambda *_: (0,) * nd, pipeline_mode=pl.Buffered(1))


def _gelu_tanh(x):
    c = -2.0 * math.sqrt(2.0 / math.pi) * LOG2E
    neg_2y_log2 = x * (c + (0.044715 * c) * (x * x))
    return x / (1.0 + jnp.exp2(neg_2y_log2))


def _in_proj_recurrent_kernel(x_ref, w_ref, convw_ref, convb_ref, wa_ref, ba_ref,
                              wi_ref, bi_ref, lam_ref,
                              q_ref, k_ref, vt_ref, rec_ref,
                              xb_ref, conv_ref, xr_carry, h_carry):
    ts = x_ref.shape[0]

    @pl.when(pl.program_id(1) == 0)
    def _():
        xr_carry[...] = jnp.zeros_like(xr_carry)
        h_carry[...] = jnp.zeros_like(h_carry)

    xb_ref[...] = x_ref[...].astype(BF16)
    row8 = lax.broadcasted_iota(jnp.int32, (V7X_SUBLANES, GATE_TILE), 0)

    def project(off, width=GATE_TILE):
        return _dot(xb_ref[...], w_ref[:, off:off + width])

    def independent_projections():
        for c in range(0, QK_WIDTH, GATE_TILE):
            q_ref[:, c:c + GATE_TILE] = (project(Q_OFF + c) * (LOG2E * HEAD_DIM ** -0.5)).astype(BF16)
            yield
        for c in range(0, QK_WIDTH, GATE_TILE):
            k_ref[:, c:c + GATE_TILE] = project(K_OFF + c).astype(BF16)
            yield
        for c in range(0, ATTN_WIDTH, GATE_TILE):
            vt_ref[c:c + GATE_TILE, :] = project(V_OFF + c).T.astype(BF16)
            yield

    def recurrent_chain(t, xr):
        cols = slice(t * GATE_TILE, (t + 1) * GATE_TILE)
        slabs = []
        for lt in range(GATE_TILE // V7X_LANES):
            slab = t * (GATE_TILE // V7X_LANES) + lt
            lanes = slice(slab * V7X_LANES, (slab + 1) * V7X_LANES)
            piece = xr[:, lt * V7X_LANES:(lt + 1) * V7X_LANES]
            conv_ref[slab, :V7X_SUBLANES, :] = xr_carry[:, lanes]
            conv_ref[slab, V7X_SUBLANES:, :] = piece
            xr_carry[:, lanes] = piece[ts - V7X_SUBLANES:, :]
            acc = piece * convw_ref[CONV_WIDTH - 1:CONV_WIDTH, lanes] + convb_ref[:, lanes]
            for shift in range(1, CONV_WIDTH):
                shifted = conv_ref[slab, pl.ds(V7X_SUBLANES - shift, ts), :]
                acc = acc + shifted * convw_ref[CONV_WIDTH - 1 - shift:CONV_WIDTH - shift, lanes]
            slabs.append(acc)
        xc = jnp.concatenate(slabs, axis=1)
        yield

        xcb = xc.astype(BF16)
        r = jax.nn.sigmoid(_dot(xcb, wa_ref[t]) + ba_ref[:, cols])
        gate_i = jax.nn.sigmoid(_dot(xcb, wi_ref[t]) + bi_ref[:, cols])
        neg_lam = -lam_ref[:, cols]
        softplus = jnp.maximum(neg_lam, 0.0) + jnp.log1p(jnp.exp(-jnp.abs(neg_lam)))
        decay = (RG_LRU_C * softplus) * r
        a = jnp.exp2(decay * (-LOG2E))
        var = jnp.tanh(decay) * (a * a + 1.0)
        std = jnp.where(var > 0.0, var * lax.rsqrt(var), 0.0)
        u = std * (gate_i * xc)
        yield

        h_prev = h_carry[:, cols]
        hs = []
        n_groups = ts // V7X_SUBLANES
        for grp in range(n_groups):
            rows = slice(grp * V7X_SUBLANES, (grp + 1) * V7X_SUBLANES)
            ag, hg = a[rows, :], u[rows, :]
            for d in (1, 2, 4):
                keep = row8 >= d
                a_s = jnp.where(keep, pltpu.roll(ag, d, 0), 1.0)
                h_s = jnp.where(keep, pltpu.roll(hg, d, 0), 0.0)
                hg = hg + ag * h_s
                ag = ag * a_s
            hg = hg + ag * h_prev
            h_prev = hg[V7X_SUBLANES - 1:, :]
            hs.append(hg)
            if (grp + 1) % (n_groups // 2) == 0:
                yield
        h_carry[:, cols] = h_prev
        h = jnp.concatenate(hs, axis=0)

        yr = project(YR_OFF + t * GATE_TILE)
        rec_ref[:, cols] = (h * _gelu_tanh(yr)).astype(BF16)
        yield

    filler = independent_projections()
    xrs = [project(XR_OFF + t * GATE_TILE) for t in range(GATE_TILES)]
    for t in range(GATE_TILES):
        for _ in recurrent_chain(t, xrs[t]):
            next(filler, None)
    for _ in filler:
        pass


def _in_proj_recurrent(x, w_in, conv_w, conv_b, wa, b_a, wi, b_i, lru_lambda):
    B, S, D = x.shape
    ts = SEQ_TILE
    row = lambda width: pl.BlockSpec((None, ts, width), lambda b, i: (b, i, 0))
    out_shape = (
        jax.ShapeDtypeStruct((B, S, QK_WIDTH), BF16),
        jax.ShapeDtypeStruct((B, S, QK_WIDTH), BF16),
        jax.ShapeDtypeStruct((B, ATTN_WIDTH, S), BF16),
        jax.ShapeDtypeStruct((B, S, REC_WIDTH), BF16),
    )
    return pl.pallas_call(
        _in_proj_recurrent_kernel,
        grid=(B, S // ts),
        in_specs=[
            row(D),
            _resident((D, G_OFF)),
            _resident((CONV_WIDTH, REC_WIDTH)),
            _resident((1, REC_WIDTH)),
            _resident((GATE_TILES, GATE_TILE, GATE_TILE)),
            _resident((1, REC_WIDTH)),
            _resident((GATE_TILES, GATE_TILE, GATE_TILE)),
            _resident((1, REC_WIDTH)),
            _resident((1, REC_WIDTH)),
        ],
        out_specs=(row(QK_WIDTH), row(QK_WIDTH),
                   pl.BlockSpec((None, ATTN_WIDTH, ts), lambda b, i: (b, 0, i)),
                   row(REC_WIDTH)),
        out_shape=out_shape,
        scratch_shapes=[
            pltpu.VMEM((ts, D), BF16),
            pltpu.VMEM((REC_WIDTH // V7X_LANES, V7X_SUBLANES + ts, V7X_LANES), F32),
            pltpu.VMEM((V7X_SUBLANES, REC_WIDTH), F32),
            pltpu.VMEM((1, REC_WIDTH), F32),
        ],
        compiler_params=pltpu.CompilerParams(
            dimension_semantics=("arbitrary", "arbitrary"),
            vmem_limit_bytes=VMEM_LIMIT["in_proj_recurrent"],
        ),
        name="in_proj_recurrent",
    )(x, w_in, conv_w, conv_b, wa, b_a, wi, b_i, lru_lambda)


def _diff_attention_kernel(lam_init, q_ref, k_ref, vt_ref, lq1_ref, lk1_ref, lq2_ref, lk2_ref, subg_ref,
                           o_ref, kpos_ref, ka0_ref, ka1_ref, vsum_ref, s0_ref, s1_ref, p0_ref, p1_ref):
    tq, tk = Q_TILE, KV_TILE
    seq = q_ref.shape[0]
    head = pl.program_id(1)

    @pl.when((pl.program_id(0) == 0) & (head == 0))
    def _():
        kpos = lax.broadcasted_iota(jnp.int32, (seq, V_DIM), 0)
        local = lax.broadcasted_iota(jnp.int32, (seq, V_DIM), 1) & (HEAD_DIM - 1)
        fine_bits = kpos & (KPOS_SPLIT - 1)
        coarse = (kpos - fine_bits).astype(F32)
        fine = fine_bits.astype(F32)
        parts = jnp.where(local < N_SLOPE_PARTS, coarse,
                          jnp.where(local < 2 * N_SLOPE_PARTS, fine, 0.0))
        kpos_ref[...] = parts.astype(BF16)

    lam = (jnp.exp(jnp.sum(lq1_ref[...] * lk1_ref[...], axis=-1, keepdims=True))
           - jnp.exp(jnp.sum(lq2_ref[...] * lk2_ref[...], axis=-1, keepdims=True))
           + lam_init)

    slope = LOG2E * jnp.exp2(-(head + 1).astype(F32) * jnp.ones((1, V_DIM), F32))
    local_row = lax.broadcasted_iota(jnp.int32, (1, V_DIM), 1) & (HEAD_DIM - 1)
    slope_parts = jnp.zeros((1, V_DIM), F32)
    rest = slope
    for part in range(N_SLOPE_PARTS):
        piece = rest.astype(BF16).astype(F32)
        pairs_with = (local_row == part) | (local_row == part + N_SLOPE_PARTS)
        slope_parts = jnp.where(pairs_with, piece, slope_parts)
        rest = rest - piece
    slope_parts = slope_parts.astype(BF16)

    lane = lax.broadcasted_iota(jnp.int32, (tq, V_DIM), 1)
    lane_k = lax.broadcasted_iota(jnp.int32, (seq, V_DIM), 1)
    k_all = k_ref[...]
    ka0_ref[...] = jnp.where(lane_k < HEAD_DIM, k_all, kpos_ref[...])
    ka1_ref[...] = jnp.where(lane_k >= HEAD_DIM, k_all, kpos_ref[...])
    slope_rows = jnp.broadcast_to(slope_parts, (tq, V_DIM))
    vsum_ref[:V_DIM, :] = vt_ref[...]
    ones_row = lax.broadcasted_iota(jnp.int32, (V_AUG - V_DIM, seq), 0) == 0
    vsum_ref[V_DIM:, :] = jnp.where(ones_row, 1.0, 0.0).astype(BF16)
    causal = (lax.broadcasted_iota(jnp.int32, (tk, tq), 0)
              <= lax.broadcasted_iota(jnp.int32, (tk, tq), 1))
    contract_last = (((1,), (1,)), ((), ()))
    subg = subg_ref[...]
    pack_rows = 2 * V7X_SUBLANES

    def row_group_reduce(op, acc, s):
        for g in range(s.shape[0] // V7X_SUBLANES):
            chunk = s[g * V7X_SUBLANES:(g + 1) * V7X_SUBLANES, :]
            acc = chunk if acc is None else op(acc, chunk)
        return acc

    n_q = seq // tq
    tiles = [dict() for _ in range(n_q)]

    def pass1_tile(i, j):
        st = tiles[i]
        if j == 0:
            q = q_ref[i * tq:(i + 1) * tq, :]
            st["q0"] = jnp.where(lane < HEAD_DIM, q, slope_rows)
            st["q1"] = jnp.where(lane >= HEAD_DIM, q, slope_rows)
            st["m0"] = st["m1"] = None
        kv = slice(j * tk, (j + 1) * tk)
        s0 = lax.dot_general(ka0_ref[kv, :], st["q0"], contract_last, preferred_element_type=F32)
        s1 = lax.dot_general(ka1_ref[kv, :], st["q1"], contract_last, preferred_element_type=F32)
        if j == i:
            s0 = jnp.where(causal, s0, -jnp.inf)
            s1 = jnp.where(causal, s1, -jnp.inf)
        s0_ref[i % N_SCORE_BUFS, j] = s0
        s1_ref[i % N_SCORE_BUFS, j] = s1
        st["m0"] = row_group_reduce(jnp.maximum, st["m0"], s0)
        st["m1"] = row_group_reduce(jnp.maximum, st["m1"], s1)

    def pass2_tile(i, j):
        st = tiles[i]
        buf = i % N_SCORE_BUFS
        if j == 0:
            for m in ("m0", "m1"):
                st[m] = jnp.broadcast_to(jnp.max(st[m], axis=0, keepdims=True), (pack_rows, tq))
        for g in range(tk // pack_rows):
            rows = slice(g * pack_rows, (g + 1) * pack_rows)
            p0_ref[buf, j, rows, :] = jnp.exp2(s0_ref[buf, j, rows, :] - st["m0"]).astype(BF16)
            p1_ref[buf, j, rows, :] = jnp.exp2(s1_ref[buf, j, rows, :] - st["m1"]).astype(BF16)

    def pass3_tile(i, j):
        st = tiles[i]
        buf = i % N_SCORE_BUFS
        kv = slice(j * tk, (j + 1) * tk)
        pv0 = _dot(vsum_ref[:, kv], p0_ref[buf, j])
        pv1 = _dot(vsum_ref[:, kv], p1_ref[buf, j])
        st["o0"] = pv0 if j == 0 else st["o0"] + pv0
        st["o1"] = pv1 if j == 0 else st["o1"] + pv1
        if j == i:
            l0 = st["o0"][V_DIM:V_DIM + 1, :]
            l1 = st["o1"][V_DIM:V_DIM + 1, :]
            o = st["o0"][:V_DIM, :] * (1.0 / l0) - st["o1"][:V_DIM, :] * (lam / l1)
            o = o * lax.rsqrt(jnp.mean(o * o, axis=0, keepdims=True) + LN_EPS)
            o = o * subg * (1.0 - lam_init)
            o_ref[i * tq:(i + 1) * tq, :] = o.T.astype(BF16)
            st.clear()

    for stage in range(n_q + 2):
        for j in range(n_q):
            for phase, fn in enumerate((pass1_tile, pass2_tile, pass3_tile)):
                i = stage - phase
                if 0 <= i < n_q and j <= i:
                    fn(i, j)


def _diff_attention(q, k, v_t, lq1, lk1, lq2, lk2, subln_g, lam_init):
    B, S, _ = q.shape
    tq, tk = Q_TILE, KV_TILE
    head_spec = pl.BlockSpec((None, S, V_DIM), lambda b, h: (b, 0, h))
    small = lambda n: pl.BlockSpec((1, n), lambda b, h: (0, 0))
    tile_stack = (N_SCORE_BUFS, S // tk, tk, tq)
    return pl.pallas_call(
        functools.partial(_diff_attention_kernel, lam_init),
        grid=(B, N_HEADS),
        in_specs=[head_spec, head_spec, pl.BlockSpec((None, V_DIM, S), lambda b, h: (b, h, 0)),
                  small(HEAD_DIM), small(HEAD_DIM), small(HEAD_DIM), small(HEAD_DIM),
                  pl.BlockSpec((V_DIM, 1), lambda b, h: (0, 0))],
        out_specs=head_spec,
        out_shape=jax.ShapeDtypeStruct((B, S, ATTN_WIDTH), BF16),
        scratch_shapes=[
            pltpu.VMEM((S, V_DIM), BF16),
            pltpu.VMEM((S, V_DIM), BF16),
            pltpu.VMEM((S, V_DIM), BF16),
            pltpu.VMEM((V_AUG, S), BF16),
            pltpu.VMEM(tile_stack, F32),
            pltpu.VMEM(tile_stack, F32),
            pltpu.VMEM(tile_stack, BF16),
            pltpu.VMEM(tile_stack, BF16),
        ],
        compiler_params=pltpu.CompilerParams(
            dimension_semantics=("arbitrary", "arbitrary"),
            vmem_limit_bytes=VMEM_LIMIT["diff_attention"],
        ),
        name="diff_attention",
    )(q, k, v_t, lq1, lk1, lq2, lk2, subln_g)


def _layer_norm(y, gain, bias):
    mu = jnp.mean(y, axis=-1, keepdims=True)
    yc = y - mu
    var = jnp.mean(yc * yc, axis=-1, keepdims=True)
    return yc * lax.rsqrt(var + LN_EPS) * gain + bias


def _merge_ffn_kernel(x_ref, attn_ref, rec_ref, wg_ref, bgate_ref, wba_ref, wbr_ref, wout_ref,
                      ln1g_ref, ln1b_ref, wgate_ref, wup_ref, wdown_ref, ln2g_ref, ln2b_ref,
                      o_ref, x1b_ref):
    def sub_tile(rows):
        a = _dot(attn_ref[rows, :], wba_ref[...])
        r = _dot(rec_ref[rows, :], wbr_ref[...])
        gate_logits = _dot(x_ref[rows, :].astype(BF16), wg_ref[...])
        yield
        g = jax.nn.sigmoid(gate_logits + bgate_ref[...])
        merged = (g[:, :D_MODEL] * a + g[:, D_MODEL:] * r).astype(BF16)
        yield
        mix = _dot(merged, wout_ref[...])
        yield
        x1 = _layer_norm(ALPHA * x_ref[rows, :] + mix, ln1g_ref[...], ln1b_ref[...])
        x1b_ref[rows, :] = x1.astype(BF16)
        yield
        ffn = None
        for lo, hi in FF_CHUNKS:
            gate = _dot(x1b_ref[rows, :], wgate_ref[:, lo:hi])
            up = _dot(x1b_ref[rows, :], wup_ref[:, lo:hi])
            yield
            hidden = (jax.nn.silu(gate) * up).astype(BF16)
            yield
            down = _dot(hidden, wdown_ref[lo:hi, :])
            ffn = down if ffn is None else ffn + down
        yield
        o_ref[rows, :] = _layer_norm(ALPHA * x1 + ffn, ln2g_ref[...], ln2b_ref[...])
        yield

    tm = x_ref.shape[0]
    slabs = [sub_tile(slice(r0, r0 + SUB_ROWS)) for r0 in range(0, tm, SUB_ROWS)]
    live = list(range(len(slabs)))
    step = 0
    while live:
        for idx in list(live):
            if step >= idx and next(slabs[idx], "done") == "done":
                live.remove(idx)
        step += 1


def _merge_ffn(x2d, attn2d, rec2d, w_g, b_gate, wba, wbr, wout, ln1_g, ln1_b, wgate, wup, wdown, ln2_g, ln2_b):
    N, D = x2d.shape
    tm = ROW_TILE
    row = lambda width: pl.BlockSpec((tm, width), lambda i: (i, 0))
    return pl.pallas_call(
        _merge_ffn_kernel,
        grid=(N // tm,),
        in_specs=[
            row(D), row(ATTN_WIDTH), row(REC_WIDTH), _resident((D, 2 * D_MODEL)), _resident((1, 2 * D_MODEL)),
            _resident((ATTN_WIDTH, D)), _resident((REC_WIDTH, D)), _resident((D, D)),
            _resident((1, D)), _resident((1, D)),
            _resident((D, D_FF)), _resident((D, D_FF)), _resident((D_FF, D)),
            _resident((1, D)), _resident((1, D)),
        ],
        out_specs=row(D),
        out_shape=jax.ShapeDtypeStruct((N, D), F32),
        scratch_shapes=[pltpu.VMEM((tm, D), BF16)],
        compiler_params=pltpu.CompilerParams(
            dimension_semantics=("arbitrary",),
            vmem_limit_bytes=VMEM_LIMIT["merge_ffn"],
        ),
        name="merge_ffn",
    )(x2d, attn2d, rec2d, w_g, b_gate, wba, wbr, wout, ln1_g, ln1_b, wgate, wup, wdown, ln2_g, ln2_b)


def _pack_block_diag(w):
    w = w.reshape(GATE_TILES, BLOCKS_PER_TILE, REC_BLOCK_DIM, REC_BLOCK_DIM)
    eye = jnp.eye(BLOCKS_PER_TILE, dtype=w.dtype)
    tiles = jnp.einsum("tbij,bc->tbicj", w, eye)
    return tiles.reshape(GATE_TILES, GATE_TILE, GATE_TILE)


def kernel(x, w_in, b_gate, lambda_q1, lambda_k1, lambda_q2, lambda_k2, subln_g, conv_w, conv_b,
           w_a, b_a, w_i, b_i, lru_lambda, w_br_attn, w_br_rec, w_out, ln1_g, ln1_b,
           w_gate, w_up, w_down, ln2_g, ln2_b):
    B, S, D = x.shape
    assert DEPTH == 1 and w_in.shape[0] == DEPTH
    l = 0
    lam_init = 0.8 - 0.6 * math.exp(-0.3 * l)
    row = lambda p: p[l].reshape(1, -1)

    w_in_b = w_in[l].astype(BF16)
    q, k, v_t, rec = _in_proj_recurrent(
        x, w_in_b[:, :G_OFF], conv_w[l], row(conv_b),
        _pack_block_diag(w_a[l]).astype(BF16), row(b_a),
        _pack_block_diag(w_i[l]).astype(BF16), row(b_i), row(lru_lambda))

    attn = _diff_attention(q, k, v_t, row(lambda_q1), row(lambda_k1), row(lambda_q2), row(lambda_k2),
                           subln_g[l].reshape(V_DIM, 1), lam_init)

    N = B * S
    out = _merge_ffn(
        x.reshape(N, D), attn.reshape(N, ATTN_WIDTH), rec.reshape(N, REC_WIDTH),
        w_in_b[:, G_OFF:], row(b_gate),
        w_br_attn[l].astype(BF16), w_br_rec[l].astype(BF16), w_out[l].astype(BF16),
        row(ln1_g), row(ln1_b),
        w_gate[l].astype(BF16), w_up[l].astype(BF16), w_down[l].astype(BF16),
        row(ln2_g), row(ln2_b))
    return out.reshape(B, S, D)
```

```python
import functools
import math

import jax
import jax.numpy as jnp
from jax import lax
from jax.experimental import pallas as pl
from jax.experimental.pallas import tpu as pltpu

D_MODEL = 1024
N_HEADS = 8
HEAD_DIM = 64
V_DIM = 2 * HEAD_DIM
QK_WIDTH = N_HEADS * 2 * HEAD_DIM
ATTN_WIDTH = N_HEADS * V_DIM
REC_WIDTH = 1024
REC_BLOCKS = 16
REC_BLOCK_DIM = REC_WIDTH // REC_BLOCKS
CONV_WIDTH = 4
RG_LRU_C = 8.0
LOG2E = math.log2(math.e)
D_FF = 2816
LN_EPS = 1e-5
DEPTH = 1
ALPHA = (2.0 * DEPTH) ** 0.25
Q_OFF = 0
K_OFF = Q_OFF + QK_WIDTH
V_OFF = K_OFF + QK_WIDTH
XR_OFF = V_OFF + ATTN_WIDTH
YR_OFF = XR_OFF + REC_WIDTH
G_OFF = YR_OFF + REC_WIDTH
IN_WIDTH = G_OFF + 2 * D_MODEL

V7X_LANES = 128
V7X_SUBLANES = 8
V7X_MXU_DIM = 256
V7X_VMEM_BYTES = 64 * 1024 * 1024

GATE_TILE = V7X_MXU_DIM
GATE_TILES = REC_WIDTH // GATE_TILE
BLOCKS_PER_TILE = GATE_TILE // REC_BLOCK_DIM

SEQ_TILE = 1024
Q_TILE = 256
KV_TILE = 256
N_SCORE_BUFS = 3
V_AUG = V_DIM + 2 * V7X_SUBLANES
N_SLOPE_PARTS = 3
KPOS_SPLIT = 16
ROW_TILE = 512
SUB_ROWS = 256
FF_CHUNKS = ((0, 1536), (1536, D_FF))

MIB = 1024 * 1024
VMEM_LIMIT = {"in_proj_recurrent": 58 * MIB, "diff_attention": 32 * MIB, "merge_ffn": 60 * MIB}
assert max(VMEM_LIMIT.values()) < V7X_VMEM_BYTES

BF16 = jnp.bfloat16
F32 = jnp.float32


def _dot(a, b):
    return jnp.dot(a, b, preferred_element_type=F32)


def _resident(shape):
    nd = len(shape)
    return pl.BlockSpec(shape, lambda *_: (0,) * nd, pipeline_mode=pl.Buffered(1))


def _gelu_tanh(x):
    c = -2.0 * math.sqrt(2.0 / math.pi) * LOG2E
    neg_2y_log2 = x * (c + (0.044715 * c) * (x * x))
    return x / (1.0 + jnp.exp2(neg_2y_log2))


def _in_proj_recurrent_kernel(x_ref, w_ref, convw_ref, convb_ref, wa_ref, ba_ref,
                              wi_ref, bi_ref, lam_ref,
                              q_ref, k_ref, vt_ref, rec_ref, g_ref,
                              xb_ref, conv_ref, xr_carry, h_carry):
    ts = x_ref.shape[0]

    @pl.when(pl.program_id(1) == 0)
    def _():
        xr_carry[...] = jnp.zeros_like(xr_carry)
        h_carry[...] = jnp.zeros_like(h_carry)

    xb_ref[...] = x_ref[...].astype(BF16)
    row8 = lax.broadcasted_iota(jnp.int32, (V7X_SUBLANES, GATE_TILE), 0)

    def project(off, width=GATE_TILE):
        return _dot(xb_ref[...], w_ref[:, off:off + width])

    def independent_projections():
        for c in range(0, QK_WIDTH, GATE_TILE):
            q_ref[:, c:c + GATE_TILE] = (project(Q_OFF + c) * (LOG2E * HEAD_DIM ** -0.5)).astype(BF16)
            yield
        for c in range(0, QK_WIDTH, GATE_TILE):
            k_ref[:, c:c + GATE_TILE] = project(K_OFF + c).astype(BF16)
            yield
        for c in range(0, ATTN_WIDTH, GATE_TILE):
            vt_ref[c:c + GATE_TILE, :] = project(V_OFF + c).T.astype(BF16)
            yield
        for c in range(0, D_MODEL, GATE_TILE):
            g_ref[:, c:c + GATE_TILE] = project(G_OFF + c)
            yield

    def recurrent_chain(t, xr):
        cols = slice(t * GATE_TILE, (t + 1) * GATE_TILE)
        slabs = []
        for lt in range(GATE_TILE // V7X_LANES):
            slab = t * (GATE_TILE // V7X_LANES) + lt
            lanes = slice(slab * V7X_LANES, (slab + 1) * V7X_LANES)
            piece = xr[:, lt * V7X_LANES:(lt + 1) * V7X_LANES]
            conv_ref[slab, :V7X_SUBLANES, :] = xr_carry[:, lanes]
            conv_ref[slab, V7X_SUBLANES:, :] = piece
            xr_carry[:, lanes] = piece[ts - V7X_SUBLANES:, :]
            acc = piece * convw_ref[CONV_WIDTH - 1:CONV_WIDTH, lanes] + convb_ref[:, lanes]
            for shift in range(1, CONV_WIDTH):
                shifted = conv_ref[slab, pl.ds(V7X_SUBLANES - shift, ts), :]
                acc = acc + shifted * convw_ref[CONV_WIDTH - 1 - shift:CONV_WIDTH - shift, lanes]
            slabs.append(acc)
        xc = jnp.concatenate(slabs, axis=1)
        yield

        xcb = xc.astype(BF16)
        r = jax.nn.sigmoid(_dot(xcb, wa_ref[t]) + ba_ref[:, cols])
        gate_i = jax.nn.sigmoid(_dot(xcb, wi_ref[t]) + bi_ref[:, cols])
        neg_lam = -lam_ref[:, cols]
        softplus = jnp.maximum(neg_lam, 0.0) + jnp.log1p(jnp.exp(-jnp.abs(neg_lam)))
        decay = (RG_LRU_C * softplus) * r
        a = jnp.exp2(decay * (-LOG2E))
        var = jnp.tanh(decay) * (a * a + 1.0)
        std = jnp.where(var > 0.0, var * lax.rsqrt(var), 0.0)
        u = std * (gate_i * xc)
        yield

        h_prev = h_carry[:, cols]
        hs = []
        n_groups = ts // V7X_SUBLANES
        for grp in range(n_groups):
            rows = slice(grp * V7X_SUBLANES, (grp + 1) * V7X_SUBLANES)
            ag, hg = a[rows, :], u[rows, :]
            for d in (1, 2, 4):
                keep = row8 >= d
                a_s = jnp.where(keep, pltpu.roll(ag, d, 0), 1.0)
                h_s = jnp.where(keep, pltpu.roll(hg, d, 0), 0.0)
                hg = hg + ag * h_s
                ag = ag * a_s
            hg = hg + ag * h_prev
            h_prev = hg[V7X_SUBLANES - 1:, :]
            hs.append(hg)
            if (grp + 1) % (n_groups // 2) == 0:
                yield
        h_carry[:, cols] = h_prev
        h = jnp.concatenate(hs, axis=0)

        yr = project(YR_OFF + t * GATE_TILE)
        rec_ref[:, cols] = (h * _gelu_tanh(yr)).astype(BF16)
        yield

    filler = independent_projections()
    xrs = [project(XR_OFF + t * GATE_TILE) for t in range(GATE_TILES)]
    for t in range(GATE_TILES):
        for _ in recurrent_chain(t, xrs[t]):
            next(filler, None)
    for _ in filler:
        pass


def _in_proj_recurrent(x, w_in, conv_w, conv_b, wa, b_a, wi, b_i, lru_lambda):
    B, S, D = x.shape
    ts = SEQ_TILE
    row = lambda width: pl.BlockSpec((None, ts, width), lambda b, i: (b, i, 0))
    out_shape = (
        jax.ShapeDtypeStruct((B, S, QK_WIDTH), BF16),
        jax.ShapeDtypeStruct((B, S, QK_WIDTH), BF16),
        jax.ShapeDtypeStruct((B, ATTN_WIDTH, S), BF16),
        jax.ShapeDtypeStruct((B, S, REC_WIDTH), BF16),
        jax.ShapeDtypeStruct((B, S, D_MODEL), F32),
    )
    return pl.pallas_call(
        _in_proj_recurrent_kernel,
        grid=(B, S // ts),
        in_specs=[
            row(D),
            _resident((D, G_OFF + D_MODEL)),
            _resident((CONV_WIDTH, REC_WIDTH)),
            _resident((1, REC_WIDTH)),
            _resident((GATE_TILES, GATE_TILE, GATE_TILE)),
            _resident((1, REC_WIDTH)),
            _resident((GATE_TILES, GATE_TILE, GATE_TILE)),
            _resident((1, REC_WIDTH)),
            _resident((1, REC_WIDTH)),
        ],
        out_specs=(row(QK_WIDTH), row(QK_WIDTH),
                   pl.BlockSpec((None, ATTN_WIDTH, ts), lambda b, i: (b, 0, i)),
                   row(REC_WIDTH), row(D_MODEL)),
        out_shape=out_shape,
        scratch_shapes=[
            pltpu.VMEM((ts, D), BF16),
            pltpu.VMEM((REC_WIDTH // V7X_LANES, V7X_SUBLANES + ts, V7X_LANES), F32),
            pltpu.VMEM((V7X_SUBLANES, REC_WIDTH), F32),
            pltpu.VMEM((1, REC_WIDTH), F32),
        ],
        compiler_params=pltpu.CompilerParams(
            dimension_semantics=("arbitrary", "arbitrary"),
            vmem_limit_bytes=VMEM_LIMIT["in_proj_recurrent"],
        ),
        name="in_proj_recurrent",
    )(x, w_in, conv_w, conv_b, wa, b_a, wi, b_i, lru_lambda)


def _diff_attention_kernel(lam_init, q_ref, k_ref, vt_ref, lq1_ref, lk1_ref, lq2_ref, lk2_ref, subg_ref,
                           o_ref, kpos_ref, ka0_ref, ka1_ref, vsum_ref, s0_ref, s1_ref, p0_ref, p1_ref):
    tq, tk = Q_TILE, KV_TILE
    seq = q_ref.shape[0]
    head = pl.program_id(1)

    @pl.when((pl.program_id(0) == 0) & (head == 0))
    def _():
        kpos = lax.broadcasted_iota(jnp.int32, (seq, V_DIM), 0)
        local = lax.broadcasted_iota(jnp.int32, (seq, V_DIM), 1) & (HEAD_DIM - 1)
        fine_bits = kpos & (KPOS_SPLIT - 1)
        coarse = (kpos - fine_bits).astype(F32)
        fine = fine_bits.astype(F32)
        parts = jnp.where(local < N_SLOPE_PARTS, coarse,
                          jnp.where(local < 2 * N_SLOPE_PARTS, fine, 0.0))
        kpos_ref[...] = parts.astype(BF16)

    lam = (jnp.exp(jnp.sum(lq1_ref[...] * lk1_ref[...], axis=-1, keepdims=True))
           - jnp.exp(jnp.sum(lq2_ref[...] * lk2_ref[...], axis=-1, keepdims=True))
           + lam_init)

    slope = LOG2E * jnp.exp2(-(head + 1).astype(F32) * jnp.ones((1, V_DIM), F32))
    local_row = lax.broadcasted_iota(jnp.int32, (1, V_DIM), 1) & (HEAD_DIM - 1)
    slope_parts = jnp.zeros((1, V_DIM), F32)
    rest = slope
    for part in range(N_SLOPE_PARTS):
        piece = rest.astype(BF16).astype(F32)
        pairs_with = (local_row == part) | (local_row == part + N_SLOPE_PARTS)
        slope_parts = jnp.where(pairs_with, piece, slope_parts)
        rest = rest - piece
    slope_parts = slope_parts.astype(BF16)

    lane = lax.broadcasted_iota(jnp.int32, (tq, V_DIM), 1)
    lane_k = lax.broadcasted_iota(jnp.int32, (seq, V_DIM), 1)
    k_all = k_ref[...]
    ka0_ref[...] = jnp.where(lane_k < HEAD_DIM, k_all, kpos_ref[...])
    ka1_ref[...] = jnp.where(lane_k >= HEAD_DIM, k_all, kpos_ref[...])
    slope_rows = jnp.broadcast_to(slope_parts, (tq, V_DIM))
    vsum_ref[:V_DIM, :] = vt_ref[...]
    ones_row = lax.broadcasted_iota(jnp.int32, (V_AUG - V_DIM, seq), 0) == 0
    vsum_ref[V_DIM:, :] = jnp.where(ones_row, 1.0, 0.0).astype(BF16)
    causal = (lax.broadcasted_iota(jnp.int32, (tk, tq), 0)
              <= lax.broadcasted_iota(jnp.int32, (tk, tq), 1))
    contract_last = (((1,), (1,)), ((), ()))
    subg = subg_ref[...]
    pack_rows = 2 * V7X_SUBLANES

    def row_group_reduce(op, acc, s):
        for g in range(s.shape[0] // V7X_SUBLANES):
            chunk = s[g * V7X_SUBLANES:(g + 1) * V7X_SUBLANES, :]
            acc = chunk if acc is None else op(acc, chunk)
        return acc

    n_q = seq // tq
    tiles = [dict() for _ in range(n_q)]

    def pass1_tile(i, j):
        st = tiles[i]
        if j == 0:
            q = q_ref[i * tq:(i + 1) * tq, :]
            st["q0"] = jnp.where(lane < HEAD_DIM, q, slope_rows)
            st["q1"] = jnp.where(lane >= HEAD_DIM, q, slope_rows)
            st["m0"] = st["m1"] = None
        kv = slice(j * tk, (j + 1) * tk)
        s0 = lax.dot_general(ka0_ref[kv, :], st["q0"], contract_last, preferred_element_type=F32)
        s1 = lax.dot_general(ka1_ref[kv, :], st["q1"], contract_last, preferred_element_type=F32)
        if j == i:
            s0 = jnp.where(causal, s0, -jnp.inf)
            s1 = jnp.where(causal, s1, -jnp.inf)
        s0_ref[i % N_SCORE_BUFS, j] = s0
        s1_ref[i % N_SCORE_BUFS, j] = s1
        st["m0"] = row_group_reduce(jnp.maximum, st["m0"], s0)
        st["m1"] = row_group_reduce(jnp.maximum, st["m1"], s1)

    def pass2_tile(i, j):
        st = tiles[i]
        buf = i % N_SCORE_BUFS
        if j == 0:
            for m in ("m0", "m1"):
                st[m] = jnp.broadcast_to(jnp.max(st[m], axis=0, keepdims=True), (pack_rows, tq))
        for g in range(tk // pack_rows):
            rows = slice(g * pack_rows, (g + 1) * pack_rows)
            p0_ref[buf, j, rows, :] = jnp.exp2(s0_ref[buf, j, rows, :] - st["m0"]).astype(BF16)
            p1_ref[buf, j, rows, :] = jnp.exp2(s1_ref[buf, j, rows, :] - st["m1"]).astype(BF16)

    def pass3_tile(i, j):
        st = tiles[i]
        buf = i % N_SCORE_BUFS
        kv = slice(j * tk, (j + 1) * tk)
        pv0 = _dot(vsum_ref[:, kv], p0_ref[buf, j])
        pv1 = _dot(vsum_ref[:, kv], p1_ref[buf, j])
        st["o0"] = pv0 if j == 0 else st["o0"] + pv0
        st["o1"] = pv1 if j == 0 else st["o1"] + pv1
        if j == i:
            l0 = st["o0"][V_DIM:V_DIM + 1, :]
            l1 = st["o1"][V_DIM:V_DIM + 1, :]
            o = st["o0"][:V_DIM, :] * (1.0 / l0) - st["o1"][:V_DIM, :] * (lam / l1)
            o = o * lax.rsqrt(jnp.mean(o * o, axis=0, keepdims=True) + LN_EPS)
            o = o * subg * (1.0 - lam_init)
            o_ref[i * tq:(i + 1) * tq, :] = o.T.astype(BF16)
            st.clear()

    for stage in range(n_q + 2):
        for j in range(n_q):
            for phase, fn in enumerate((pass1_tile, pass2_tile, pass3_tile)):
                i = stage - phase
                if 0 <= i < n_q and j <= i:
                    fn(i, j)


def _diff_attention(q, k, v_t, lq1, lk1, lq2, lk2, subln_g, lam_init):
    B, S, _ = q.shape
    tq, tk = Q_TILE, KV_TILE
    head_spec = pl.BlockSpec((None, S, V_DIM), lambda b, h: (b, 0, h))
    small = lambda n: pl.BlockSpec((1, n), lambda b, h: (0, 0))
    tile_stack = (N_SCORE_BUFS, S // tk, tk, tq)
    return pl.pallas_call(
        functools.partial(_diff_attention_kernel, lam_init),
        grid=(B, N_HEADS),
        in_specs=[head_spec, head_spec, pl.BlockSpec((None, V_DIM, S), lambda b, h: (b, h, 0)),
                  small(HEAD_DIM), small(HEAD_DIM), small(HEAD_DIM), small(HEAD_DIM),
                  pl.BlockSpec((V_DIM, 1), lambda b, h: (0, 0))],
        out_specs=head_spec,
        out_shape=jax.ShapeDtypeStruct((B, S, ATTN_WIDTH), BF16),
        scratch_shapes=[
            pltpu.VMEM((S, V_DIM), BF16),
            pltpu.VMEM((S, V_DIM), BF16),
            pltpu.VMEM((S, V_DIM), BF16),
            pltpu.VMEM((V_AUG, S), BF16),
            pltpu.VMEM(tile_stack, F32),
            pltpu.VMEM(tile_stack, F32),
            pltpu.VMEM(tile_stack, BF16),
            pltpu.VMEM(tile_stack, BF16),
        ],
        compiler_params=pltpu.CompilerParams(
            dimension_semantics=("arbitrary", "arbitrary"),
            vmem_limit_bytes=VMEM_LIMIT["diff_attention"],
        ),
        name="diff_attention",
    )(q, k, v_t, lq1, lk1, lq2, lk2, subln_g)


def _layer_norm(y, gain, bias):
    mu = jnp.mean(y, axis=-1, keepdims=True)
    yc = y - mu
    var = jnp.mean(yc * yc, axis=-1, keepdims=True)
    return yc * lax.rsqrt(var + LN_EPS) * gain + bias


def _merge_ffn_kernel(x_ref, attn_ref, rec_ref, gl_ref, wg_ref, bgate_ref, wba_ref, wbr_ref, wout_ref,
                      ln1g_ref, ln1b_ref, wgate_ref, wup_ref, wdown_ref, ln2g_ref, ln2b_ref,
                      o_ref, x1b_ref):
    def sub_tile(rows):
        a = _dot(attn_ref[rows, :], wba_ref[...])
        r = _dot(rec_ref[rows, :], wbr_ref[...])
        rec_logits = _dot(x_ref[rows, :].astype(BF16), wg_ref[...])
        yield
        g_attn = jax.nn.sigmoid(gl_ref[rows, :] + bgate_ref[:, :D_MODEL])
        g_rec = jax.nn.sigmoid(rec_logits + bgate_ref[:, D_MODEL:])
        merged = (g_attn * a + g_rec * r).astype(BF16)
        yield
        mix = _dot(merged, wout_ref[...])
        yield
        x1 = _layer_norm(ALPHA * x_ref[rows, :] + mix, ln1g_ref[...], ln1b_ref[...])
        x1b_ref[rows, :] = x1.astype(BF16)
        yield
        ffn = None
        for lo, hi in FF_CHUNKS:
            gate = _dot(x1b_ref[rows, :], wgate_ref[:, lo:hi])
            up = _dot(x1b_ref[rows, :], wup_ref[:, lo:hi])
            yield
            hidden = (jax.nn.silu(gate) * up).astype(BF16)
            yield
            down = _dot(hidden, wdown_ref[lo:hi, :])
            ffn = down if ffn is None else ffn + down
        yield
        o_ref[rows, :] = _layer_norm(ALPHA * x1 + ffn, ln2g_ref[...], ln2b_ref[...])
        yield

    tm = x_ref.shape[0]
    slabs = [sub_tile(slice(r0, r0 + SUB_ROWS)) for r0 in range(0, tm, SUB_ROWS)]
    live = list(range(len(slabs)))
    step = 0
    while live:
        for idx in list(live):
            if step >= idx and next(slabs[idx], "done") == "done":
                live.remove(idx)
        step += 1


def _merge_ffn(x2d, attn2d, rec2d, gl2d, w_g, b_gate, wba, wbr, wout, ln1_g, ln1_b, wgate, wup, wdown, ln2_g, ln2_b):
    N, D = x2d.shape
    tm = ROW_TILE
    row = lambda width: pl.BlockSpec((tm, width), lambda i: (i, 0))
    return pl.pallas_call(
        _merge_ffn_kernel,
        grid=(N // tm,),
        in_specs=[
            row(D), row(ATTN_WIDTH), row(REC_WIDTH), row(D), _resident((D, D_MODEL)), _resident((1, 2 * D_MODEL)),
            _resident((ATTN_WIDTH, D)), _resident((REC_WIDTH, D)), _resident((D, D)),
            _resident((1, D)), _resident((1, D)),
            _resident((D, D_FF)), _resident((D, D_FF)), _resident((D_FF, D)),
            _resident((1, D)), _resident((1, D)),
        ],
        out_specs=row(D),
        out_shape=jax.ShapeDtypeStruct((N, D), F32),
        scratch_shapes=[pltpu.VMEM((tm, D), BF16)],
        compiler_params=pltpu.CompilerParams(
            dimension_semantics=("arbitrary",),
            vmem_limit_bytes=VMEM_LIMIT["merge_ffn"],
        ),
        name="merge_ffn",
    )(x2d, attn2d, rec2d, gl2d, w_g, b_gate, wba, wbr, wout, ln1_g, ln1_b, wgate, wup, wdown, ln2_g, ln2_b)


def _pack_block_diag(w):
    w = w.reshape(GATE_TILES, BLOCKS_PER_TILE, REC_BLOCK_DIM, REC_BLOCK_DIM)
    eye = jnp.eye(BLOCKS_PER_TILE, dtype=w.dtype)
    tiles = jnp.einsum("tbij,bc->tbicj", w, eye)
    return tiles.reshape(GATE_TILES, GATE_TILE, GATE_TILE)


def kernel(x, w_in, b_gate, lambda_q1, lambda_k1, lambda_q2, lambda_k2, subln_g, conv_w, conv_b,
           w_a, b_a, w_i, b_i, lru_lambda, w_br_attn, w_br_rec, w_out, ln1_g, ln1_b,
           w_gate, w_up, w_down, ln2_g, ln2_b):
    B, S, D = x.shape
    assert DEPTH == 1 and w_in.shape[0] == DEPTH
    l = 0
    lam_init = 0.8 - 0.6 * math.exp(-0.3 * l)
    row = lambda p: p[l].reshape(1, -1)

    w_in_b = w_in[l].astype(BF16)
    q, k, v_t, rec, attn_gate_logits = _in_proj_recurrent(
        x, w_in_b[:, :G_OFF + D_MODEL], conv_w[l], row(conv_b),
        _pack_block_diag(w_a[l]).astype(BF16), row(b_a),
        _pack_block_diag(w_i[l]).astype(BF16), row(b_i), row(lru_lambda))

    attn = _diff_attention(q, k, v_t, row(lambda_q1), row(lambda_k1), row(lambda_q2), row(lambda_k2),
                           subln_g[l].reshape(V_DIM, 1), lam_init)

    N = B * S
    out = _merge_ffn(
        x.reshape(N, D), attn.reshape(N, ATTN_WIDTH), rec.reshape(N, REC_WIDTH),
        attn_gate_logits.reshape(N, D_MODEL), w_in_b[:, G_OFF + D_MODEL:], row(b_gate),
        w_br_attn[l].astype(BF16), w_br_rec[l].astype(BF16), w_out[l].astype(BF16),
        row(ln1_g), row(ln1_b),
        w_gate[l].astype(BF16), w_up[l].astype(BF16), w_down[l].astype(BF16),
        row(ln2_g), row(ln2_b))
    return out.reshape(B, S, D)
```
